```python
import jax
import jax.numpy as jnp
from jax import lax
import numpy as np

D_MODEL = 1024
BATCH = 4
SEQ = 4096
DEPTH = 1

CHUNK = 64
N_META = 16
PAD_FRONT = (-N_META) % CHUNK
CONV_DIM = D_MODEL
CONV_WIDTH = 31
M_HEADS = 4
M_DV = D_MODEL // M_HEADS
M_DK = M_DV // 2
M_DIM = M_HEADS * M_DV
QK_SCALE = M_DK ** -0.5
N_EXPERTS = 32
TOP_K = 4
D_FF = D_MODEL
SWIGLU_ALPHA = 1.702
SWIGLU_LIMIT = 7.0
MOE_BLOCK = 128
NORM_EPS = 1e-5
NEG_GATE = -1.0e4

GLU_OFF = 0
Q_OFF = GLU_OFF + 2 * CONV_DIM
K_OFF = Q_OFF + M_HEADS * M_DK
V_OFF = K_OFF + M_HEADS * M_DK
I_OFF = V_OFF + M_DIM
F_OFF = I_OFF + M_HEADS
O_OFF = F_OFF + M_HEADS
GC_OFF = O_OFF + M_DIM
GM_OFF = GC_OFF + D_MODEL
N_IN = GM_OFF + D_MODEL

kernel_name = 'hybrid_conformer_mlstm_moe_block'


def rms_norm(x, g):
    xf = x.astype(jnp.float32)
    y = xf * lax.rsqrt(jnp.mean(xf * xf, axis=-1, keepdims=True) + NORM_EPS)
    return (y * g.astype(jnp.float32)).astype(x.dtype)


def layer_norm(x, g, b):
    xf = x.astype(jnp.float32)
    mu = jnp.mean(xf, axis=-1, keepdims=True)
    var = jnp.mean(jnp.square(xf - mu), axis=-1, keepdims=True)
    y = (xf - mu) * lax.rsqrt(var + NORM_EPS)
    return (y * g.astype(jnp.float32) + b.astype(jnp.float32)).astype(x.dtype)


def conformer_conv(zc, dw_w, dw_b, ln_g, ln_b, pw_w, pw_b):
    a, gate = jnp.split(zc, 2, axis=-1)
    y = a * jax.nn.sigmoid(gate)
    y = lax.conv_general_dilated(
        y, dw_w[:, None, :].astype(y.dtype), window_strides=(1,),
        padding=[(CONV_WIDTH - 1, 0)], dimension_numbers=('NWC', 'WIO', 'NWC'),
        feature_group_count=CONV_DIM) + dw_b
    y = jax.nn.silu(layer_norm(y, ln_g, ln_b))
    return y @ pw_w + pw_b


def mlstm_chunkwise(q, k, v, ig, lf):
    bsz, nh, length, dk = q.shape
    dv = v.shape[-1]
    nc = length // CHUNK
    q = q.reshape(bsz, nh, nc, CHUNK, dk)
    k = k.reshape(bsz, nh, nc, CHUNK, dk)
    v = v.reshape(bsz, nh, nc, CHUNK, dv)
    ig = ig.reshape(bsz, nh, nc, CHUNK)
    b = jnp.cumsum(lf.reshape(bsz, nh, nc, CHUNK), axis=-1)
    g = b[..., -1]
    a = g[..., None] - b + ig
    m_loc = jnp.max(a, axis=-1)
    w = jnp.exp(a - m_loc[..., None])
    c_loc = jnp.einsum('bhcs,bhcsv,bhcsk->bhcvk', w, v, k)
    n_loc = jnp.einsum('bhcs,bhcsk->bhck', w, k)

    def step(carry, inp):
        c_st, n_st, m_st = carry
        c_l, n_l, m_l, g_c = inp
        m_new = jnp.maximum(g_c + m_st, m_l)
        s_old = jnp.exp(g_c + m_st - m_new)
        s_loc = jnp.exp(m_l - m_new)
        c_new = s_old[..., None, None] * c_st + s_loc[..., None, None] * c_l
        n_new = s_old[..., None] * n_st + s_loc[..., None] * n_l
        return (c_new, n_new, m_new), (c_st, n_st, m_st)

    init = (jnp.zeros((bsz, nh, dv, dk), jnp.float32),
            jnp.zeros((bsz, nh, dk), jnp.float32),
            jnp.zeros((bsz, nh), jnp.float32))
    xs = (jnp.moveaxis(c_loc, 2, 0), jnp.moveaxis(n_loc, 2, 0),
          jnp.moveaxis(m_loc, 2, 0), jnp.moveaxis(g, 2, 0))
    _, (c_prev, n_prev, m_prev) = lax.scan(step, init, xs)
    c_prev = jnp.moveaxis(c_prev, 0, 2)
    n_prev = jnp.moveaxis(n_prev, 0, 2)
    m_prev = jnp.moveaxis(m_prev, 0, 2)

    causal = jnp.tril(jnp.ones((CHUNK, CHUNK), dtype=bool))
    d_log = jnp.where(causal, b[..., :, None] - b[..., None, :] + ig[..., None, :], -jnp.inf)
    inter_log = b + m_prev[..., None]
    m_t = jnp.maximum(inter_log, jnp.max(d_log, axis=-1))
    s = jnp.einsum('bhctk,bhcsk->bhcts', q, k) * jnp.exp(d_log - m_t[..., None])
    inter_sc = jnp.exp(inter_log - m_t)
    num = (jnp.einsum('bhcts,bhcsv->bhctv', s, v)
           + inter_sc[..., None] * jnp.einsum('bhctk,bhcvk->bhctv', q, c_prev))
    den = jnp.sum(s, axis=-1) + inter_sc * jnp.einsum('bhctk,bhck->bhct', q, n_prev)
    h = num / jnp.maximum(jnp.abs(den), jnp.exp(-m_t))[..., None]
    return h.reshape(bsz, nh, length, dv)


def mlstm_branch(z, norm_g, out_w):
    bsz, length, _ = z.shape

    def to_heads(cols, dh):
        return cols.reshape(bsz, length, M_HEADS, dh).transpose(0, 2, 1, 3).astype(jnp.float32)

    q = to_heads(z[..., Q_OFF:K_OFF], M_DK) * QK_SCALE
    k = to_heads(z[..., K_OFF:V_OFF], M_DK)
    v = to_heads(z[..., V_OFF:I_OFF], M_DV)
    ig = z[..., I_OFF:F_OFF].astype(jnp.float32).transpose(0, 2, 1)
    lf = jax.nn.log_sigmoid(z[..., F_OFF:O_OFF].astype(jnp.float32)).transpose(0, 2, 1)
    pad4 = ((0, 0), (0, 0), (PAD_FRONT, 0), (0, 0))
    pad3 = pad4[:3]
    h = mlstm_chunkwise(jnp.pad(q, pad4), jnp.pad(k, pad4), jnp.pad(v, pad4),
                        jnp.pad(ig, pad3, constant_values=NEG_GATE), jnp.pad(lf, pad3))
    h = h[:, :, PAD_FRONT:].transpose(0, 2, 1, 3)
    h = h * lax.rsqrt(jnp.mean(h * h, axis=-1, keepdims=True) + NORM_EPS)
    h = h.reshape(bsz, length, M_DIM) * norm_g.astype(jnp.float32)
    o = jax.nn.sigmoid(z[..., O_OFF:GC_OFF])
    return (o * h.astype(z.dtype)) @ out_w


def moe_ffn(xt, router_w, router_b, w1, b1, w2, b2):
    n_tok, d = xt.shape
    logits = xt.astype(jnp.float32) @ router_w.astype(jnp.float32) + router_b.astype(jnp.float32)
    top_val, top_idx = lax.top_k(logits, TOP_K)
    gates = jax.nn.softmax(top_val, axis=-1).astype(xt.dtype)
    n_asg = n_tok * TOP_K
    flat_e = top_idx.reshape(-1)
    order = jnp.argsort(flat_e)
    sorted_e = flat_e[order]
    counts = jnp.bincount(flat_e, length=N_EXPERTS)
    padded = (counts + MOE_BLOCK - 1) // MOE_BLOCK * MOE_BLOCK
    grp_start = jnp.cumsum(counts) - counts
    pad_end = jnp.cumsum(padded)
    pad_start = pad_end - padded
    dest_sorted = (pad_start[sorted_e] + jnp.arange(n_asg) - grp_start[sorted_e]).astype(jnp.int32)
    n_blocks = -(-n_asg // MOE_BLOCK) + N_EXPERTS
    slot_tok = jnp.full((n_blocks * MOE_BLOCK,), n_tok, jnp.int32).at[dest_sorted].set(
        (order // TOP_K).astype(jnp.int32))
    blk_e = jnp.minimum(jnp.searchsorted(pad_end, jnp.arange(n_blocks) * MOE_BLOCK, side='right'),
                        N_EXPERTS - 1)
    x_ext = jnp.concatenate([xt, jnp.zeros((1, d), xt.dtype)], axis=0)
    xs = x_ext[slot_tok].reshape(n_blocks, MOE_BLOCK, d)

    def expert_block(args):
        xb, e = args
        hcat = xb @ w1[e] + b1[e]
        h_glu, h_lin = jnp.split(hcat, 2, axis=-1)
        h_glu = jnp.minimum(h_glu, SWIGLU_LIMIT)
        h_lin = jnp.clip(h_lin, -SWIGLU_LIMIT, SWIGLU_LIMIT)
        act = h_glu * jax.nn.sigmoid(SWIGLU_ALPHA * h_glu) * (h_lin + 1.0)
        return act @ w2[e] + b2[e]

    ys = lax.map(expert_block, (xs, blk_e)).reshape(n_blocks * MOE_BLOCK, d)
    dest = jnp.zeros((n_asg,), jnp.int32).at[order].set(dest_sorted).reshape(n_tok, TOP_K)
    return jnp.einsum('nk,nkd->nd', gates, ys[dest])


def setup_inputs(seed: int = 0) -> dict:
    key = jax.random.key(seed)
    ks = jax.random.split(key, 24)
    D = D_MODEL
    nrm = jax.random.normal
    b_in = 0.02 * nrm(ks[4], (DEPTH, N_IN), jnp.float32)
    b_in = b_in.at[:, F_OFF:O_OFF].add(jnp.linspace(3.0, 6.0, M_HEADS))
    return {
        'x': nrm(ks[0], (BATCH, SEQ, D), jnp.float32),
        'meta_tokens': nrm(ks[1], (N_META, D), jnp.float32),
        'norm_mix_g': 1.0 + 0.02 * nrm(ks[2], (DEPTH, D), jnp.float32),
        'w_in': nrm(ks[3], (DEPTH, D, N_IN), jnp.float32) * D ** -0.5,
        'b_in': b_in,
        'conv_dw_w': nrm(ks[5], (DEPTH, CONV_WIDTH, CONV_DIM), jnp.float32) * CONV_WIDTH ** -0.5,
        'conv_dw_b': 0.02 * nrm(ks[6], (DEPTH, CONV_DIM), jnp.float32),
        'conv_ln_g': 1.0 + 0.02 * nrm(ks[7], (DEPTH, CONV_DIM), jnp.float32),
        'conv_ln_b': 0.02 * nrm(ks[8], (DEPTH, CONV_DIM), jnp.float32),
        'conv_pw_w': nrm(ks[9], (DEPTH, CONV_DIM, D), jnp.float32) * CONV_DIM ** -0.5,
        'conv_pw_b': 0.02 * nrm(ks[10], (DEPTH, D), jnp.float32),
        'mlstm_norm_g': 1.0 + 0.02 * nrm(ks[11], (DEPTH, M_DIM), jnp.float32),
        'mlstm_out_w': nrm(ks[12], (DEPTH, M_DIM, D), jnp.float32) * M_DIM ** -0.5,
        'w_out': nrm(ks[13], (DEPTH, D, D), jnp.float32) * D ** -0.5,
        'norm_ffn_g': 1.0 + 0.02 * nrm(ks[14], (DEPTH, D), jnp.float32),
        'router_w': nrm(ks[15], (DEPTH, D, N_EXPERTS), jnp.float32) * D ** -0.5,
        'router_b': 0.01 * nrm(ks[16], (DEPTH, N_EXPERTS), jnp.float32),
        'expert_w1': nrm(ks[17], (DEPTH, N_EXPERTS, D, 2 * D_FF), jnp.float32) * D ** -0.5,
        'expert_b1': 0.02 * nrm(ks[18], (DEPTH, N_EXPERTS, 2 * D_FF), jnp.float32),
        'expert_w2': nrm(ks[19], (DEPTH, N_EXPERTS, D_FF, D), jnp.float32) * D_FF ** -0.5,
        'expert_b2': 0.02 * nrm(ks[20], (DEPTH, N_EXPERTS, D), jnp.float32),
        'final_norm_g': 1.0 + 0.02 * nrm(ks[21], (D,), jnp.float32),
    }


def reference(x, meta_tokens, norm_mix_g, w_in, b_in, conv_dw_w, conv_dw_b, conv_ln_g, conv_ln_b,
              conv_pw_w, conv_pw_b, mlstm_norm_g, mlstm_out_w, w_out, norm_ffn_g, router_w, router_b,
              expert_w1, expert_b1, expert_w2, expert_b2, final_norm_g):
    bsz, seq, d = x.shape
    meta = jnp.broadcast_to(meta_tokens.astype(x.dtype)[None], (bsz, N_META, d))
    h = jnp.concatenate([meta, x], axis=1)
    for l in range(DEPTH):
        u = rms_norm(h, norm_mix_g[l])
        z = u @ w_in[l] + b_in[l]
        conv_out = conformer_conv(z[..., GLU_OFF:Q_OFF], conv_dw_w[l], conv_dw_b[l], conv_ln_g[l],
                                  conv_ln_b[l], conv_pw_w[l], conv_pw_b[l])
        mlstm_out = mlstm_branch(z, mlstm_norm_g[l], mlstm_out_w[l])
        g_conv = jax.nn.sigmoid(z[..., GC_OFF:GM_OFF])
        g_mlstm = jax.nn.sigmoid(z[..., GM_OFF:N_IN])
        h = h + (g_conv * conv_out + g_mlstm * mlstm_out) @ w_out[l]
        u2 = rms_norm(h, norm_ffn_g[l])
        h = h + moe_ffn(u2.reshape(-1, d), router_w[l], router_b[l], expert_w1[l], expert_b1[l],
                        expert_w2[l], expert_b2[l]).reshape(h.shape)
    return rms_norm(h, final_norm_g)[:, N_META:]
```

```python
import functools

import jax
import jax.numpy as jnp
from jax import lax
from jax.experimental import pallas as pl
from jax.experimental.pallas import tpu as pltpu

F32 = jnp.float32
BF16 = jnp.bfloat16

D_MODEL = 1024
N_META = 16
CONV_WIDTH = 31
M_HEADS = 4
M_DK = 128
M_DV = 256
QK_SCALE = M_DK ** -0.5
N_EXPERTS = 32
TOP_K = 4
D_FF = D_MODEL
SWIGLU_ALPHA = 1.702
SWIGLU_LIMIT = 7.0
NORM_EPS = 1e-5
NEG_GATE = -1.0e4

LANES = 128
SUBLANES = 8
VMEM_LIMIT = 56 * 1024 * 1024

SEQ_TILE = 256
CONV_HIST = 32
CONV_ROWS = 64
MOE_BM = 128
COMB_TILE = 128

C_GLU = 0
C_Q = C_GLU + 2 * D_MODEL
C_K = C_Q + M_HEADS * M_DK
C_V = C_K + M_HEADS * M_DK
C_O = C_V + M_HEADS * M_DV
C_GC = C_O + D_MODEL
C_GM = C_GC + D_MODEL
C_IF = C_GM + D_MODEL
N_PROJ = C_IF + LANES


def _rms_norm(x, g):
    return x * lax.rsqrt(jnp.mean(x * x, axis=-1, keepdims=True) + NORM_EPS) * g


def _proj(u_bf, w, lo, hi):
    return jnp.dot(u_bf, w["win"][:, lo:hi], preferred_element_type=F32) + w["bin"][:, lo:hi]


def _glu(u_bf, w, row_valid):
    zg = _proj(u_bf, w, C_GLU, C_Q)
    y = zg[:, :D_MODEL] * jax.nn.sigmoid(zg[:, D_MODEL:])
    if row_valid is not None:
        y = jnp.where(row_valid, y, 0.0)
    return y


def _depthwise_conv(ybuf, dww_ref, dwb_ref):
    first = CONV_HIST - (CONV_WIDTH - 1)
    row_blocks = []
    for c in range(SEQ_TILE // CONV_ROWS):
        col_blocks = []
        for j in range(D_MODEL // LANES):
            cs = slice(j * LANES, (j + 1) * LANES)
            win = ybuf[c * CONV_ROWS:c * CONV_ROWS + CONV_ROWS + CONV_HIST, cs]
            acc = jnp.broadcast_to(dwb_ref[:, cs], (CONV_ROWS, LANES))
            for phase in range(SUBLANES):
                taps = [k for k in range(CONV_WIDTH) if (-(first + k)) % SUBLANES == phase]
                if not taps:
                    continue
                wb = win if phase == 0 else pltpu.roll(win, phase, axis=0)
                for k in taps:
                    i0 = first + k + phase
                    acc = acc + dww_ref[k:k + 1, cs] * wb[i0:i0 + CONV_ROWS, :]
            col_blocks.append(acc)
        row_blocks.append(jnp.concatenate(col_blocks, axis=1))
    return jnp.concatenate(row_blocks, axis=0)


def _mlstm(u_bf, w, c_st, n_st, m_st, row_valid, want_h):
    t = SEQ_TILE
    zqkv = _proj(u_bf, w, C_Q, C_O)
    zif = _proj(u_bf, w, C_IF, N_PROJ)
    ig_all = zif
    lf_all = jax.nn.log_sigmoid(zif)
    if row_valid is not None:
        ig_all = jnp.where(row_valid, ig_all, NEG_GATE)
        lf_all = jnp.where(row_valid, lf_all, 0.0)
    rows = lax.broadcasted_iota(jnp.int32, (t, t), 0)
    cols = lax.broadcasted_iota(jnp.int32, (t, t), 1)
    causal = cols <= rows
    b_all = jnp.dot(causal.astype(F32), lf_all, precision=lax.Precision.HIGHEST,
                    preferred_element_type=F32)
    b_sh = pltpu.roll(b_all, LANES - M_HEADS, axis=1)
    a_all = ig_all - b_sh
    a_rows = a_all.T
    hs = []
    for h in range(M_HEADS):
        a_row = a_rows[h:h + 1, :]
        a_col = a_all[:, h:h + 1]
        b_col = b_sh[:, h:h + 1]
        m_prev = m_st[h][0:1, 0:1]
        n_prev = n_st[h][0:1, :]
        ct = c_st[h]
        k = zqkv[:, M_HEADS * M_DK + h * M_DK:M_HEADS * M_DK + (h + 1) * M_DK]
        v = zqkv[:, 2 * M_HEADS * M_DK + h * M_DV:2 * M_HEADS * M_DK + (h + 1) * M_DV]
        v_bf = v.astype(BF16)
        amat = jnp.where(causal, a_row, -jnp.inf)
        m_run = jnp.maximum(jnp.max(amat, axis=-1, keepdims=True), m_prev)
        if want_h:
            q = zqkv[:, h * M_DK:(h + 1) * M_DK] * QK_SCALE
            q_bf = q.astype(BF16)
            s = lax.dot_general(q_bf, k.astype(BF16), (((1,), (1,)), ((), ())),
                                preferred_element_type=F32)
            p = s * jnp.exp(amat - m_run)
            inter_sc = jnp.exp(m_prev - m_run)
            num = (jnp.dot(p.astype(BF16), v_bf, preferred_element_type=F32)
                   + inter_sc * jnp.dot(q_bf, ct.astype(BF16), preferred_element_type=F32))
            den = (jnp.sum(p, axis=-1, keepdims=True)
                   + inter_sc * jnp.sum(q * n_prev, axis=-1, keepdims=True))
            scale = 1.0 / jnp.maximum(jnp.abs(den), jnp.exp(-(b_col + m_run)))
            hh = num * scale
            hh = hh * lax.rsqrt(jnp.mean(hh * hh, axis=-1, keepdims=True) + NORM_EPS)
            hs.append(hh * w["mng"][:, h * M_DV:(h + 1) * M_DV])
        m_last = m_run[t - 1:t, :]
        decay = jnp.exp(m_prev - m_last)
        kw = k * jnp.exp(a_col - m_last)
        c_st[h] = decay * ct + jnp.dot(kw.T.astype(BF16), v_bf, preferred_element_type=F32)
        n_new = decay * n_prev + jnp.sum(kw, axis=0, keepdims=True)
        n_st[h] = jnp.broadcast_to(n_new, (SUBLANES, M_DK))
        m_st[h] = jnp.broadcast_to(b_col[t - 1:t, :] + m_last, (SUBLANES, LANES))
    return jnp.concatenate(hs, axis=1) if want_h else None


def _router(u2, rw_ref, rb_ref):
    t = u2.shape[0]
    logits = jnp.dot(u2, rw_ref[...], precision=lax.Precision.HIGHEST,
                     preferred_element_type=F32) + rb_ref[...]
    lane = lax.broadcasted_iota(jnp.int32, (t, LANES), 1)
    cur = jnp.where(lane < N_EXPERTS, logits, -jnp.inf)
    idx_out = jnp.zeros((t, LANES), jnp.int32)
    val_out = jnp.zeros((t, LANES), F32)
    top0 = None
    esum = None
    for kk in range(TOP_K):
        mx = jnp.max(cur, axis=-1, keepdims=True)
        ix = jnp.min(jnp.where(cur == mx, lane, LANES), axis=-1, keepdims=True)
        if kk == 0:
            top0 = mx
        ek = jnp.exp(mx - top0)
        esum = ek if kk == 0 else esum + ek
        idx_out = jnp.where(lane == kk, ix, idx_out)
        val_out = jnp.where(lane == kk, ek, val_out)
        cur = jnp.where(lane == ix, -jnp.inf, cur)
    return idx_out, val_out / esum


def _weights(refs):
    names = ("g1", "win", "bin", "dww", "dwb", "lng", "lnb", "pww", "pwb", "mng", "mow",
             "wout", "g2", "rw", "rb")
    return dict(zip(names, refs))


N_WEIGHTS = 15


def _prefix_kernel(*refs):
    x_ref = refs[0]
    w = _weights(refs[1:1 + N_WEIGHTS])
    ytail_ref, c_out, n_out, m_out = refs[1 + N_WEIGHTS:]
    t = SEQ_TILE
    row_valid = lax.broadcasted_iota(jnp.int32, (t, 1), 0) >= t - N_META
    c_out[...] = jnp.zeros(c_out.shape, F32)
    n_out[...] = jnp.zeros(n_out.shape, F32)
    m_out[...] = jnp.zeros(m_out.shape, F32)
    u_bf = _rms_norm(x_ref[...], w["g1"][...]).astype(BF16)
    y = _glu(u_bf, w, row_valid)
    ytail_ref[...] = y[t - CONV_HIST:, :]
    _mlstm(u_bf, w, c_out, n_out, m_out, row_valid, want_h=False)


def _mixer_kernel(*refs):
    x_ref = refs[0]
    w = _weights(refs[1:1 + N_WEIGHTS])
    y0_ref, c0_ref, n0_ref, m0_ref = refs[1 + N_WEIGHTS:5 + N_WEIGHTS]
    h1_ref, u2_ref, idx_ref, gate_ref = refs[5 + N_WEIGHTS:9 + N_WEIGHTS]
    ybuf, c_st, n_st, m_st = refs[9 + N_WEIGHTS:]
    t = SEQ_TILE

    @pl.when(pl.program_id(1) == 0)
    def _():
        ybuf[0:CONV_HIST, :] = y0_ref[...]
        c_st[...] = c0_ref[...]
        n_st[...] = n0_ref[...]
        m_st[...] = m0_ref[...]

    x = x_ref[0]
    u_bf = _rms_norm(x, w["g1"][...]).astype(BF16)

    ybuf[CONV_HIST:CONV_HIST + t, :] = _glu(u_bf, w, None)
    conv = _depthwise_conv(ybuf, w["dww"], w["dwb"])
    ybuf[0:CONV_HIST, :] = ybuf[t:t + CONV_HIST, :]
    mu = jnp.mean(conv, axis=-1, keepdims=True)
    cen = conv - mu
    var = jnp.mean(cen * cen, axis=-1, keepdims=True)
    ln = cen * lax.rsqrt(var + NORM_EPS) * w["lng"][...] + w["lnb"][...]
    conv_out = jnp.dot(jax.nn.silu(ln).astype(BF16), w["pww"][...],
                       preferred_element_type=F32) + w["pwb"][...]

    hcat = _mlstm(u_bf, w, c_st, n_st, m_st, None, want_h=True)
    o_gate = jax.nn.sigmoid(_proj(u_bf, w, C_O, C_GC))
    mlstm_out = jnp.dot((o_gate * hcat).astype(BF16), w["mow"][...], preferred_element_type=F32)

    g_conv = jax.nn.sigmoid(_proj(u_bf, w, C_GC, C_GM))
    g_mlstm = jax.nn.sigmoid(_proj(u_bf, w, C_GM, C_IF))
    mix = (g_conv * conv_out + g_mlstm * mlstm_out).astype(BF16)
    h1 = x + jnp.dot(mix, w["wout"][...], preferred_element_type=F32)
    h1_ref[0] = h1

    u2 = _rms_norm(h1, w["g2"][...])
    u2_ref[0] = u2
    idx, gates = _router(u2, w["rw"], w["rb"])
    idx_ref[0] = idx
    gate_ref[0] = gates


def _const_spec(shape):
    nd = len(shape)
    return pl.BlockSpec(shape, lambda *_: (0,) * nd, pipeline_mode=pl.Buffered(1))


def _mixer_weights(norm_mix_g, w_in, b_in, conv_dw_w, conv_dw_b, conv_ln_g, conv_ln_b, conv_pw_w,
                   conv_pw_b, mlstm_norm_g, mlstm_out_w, w_out, norm_ffn_g, router_w, router_b):
    d = D_MODEL
    qo, ko, vo = 2 * d, 2 * d + 512, 2 * d + 1024
    io = vo + d
    oo = io + 2 * M_HEADS

    def regroup(a):
        parts = [a[..., 0:qo], a[..., qo:ko], a[..., ko:vo], a[..., vo:io],
                 a[..., oo:oo + d], a[..., oo + d:oo + 2 * d], a[..., oo + 2 * d:oo + 3 * d],
                 a[..., io:oo],
                 jnp.zeros(a.shape[:-1] + (LANES - 2 * M_HEADS,), a.dtype)]
        return jnp.concatenate(parts, axis=-1)

    row = lambda a: a.reshape(1, -1).astype(F32)
    dww = jnp.pad(conv_dw_w[0], ((0, 32 - CONV_WIDTH), (0, 0)))
    rw = jnp.pad(router_w[0], ((0, 0), (0, LANES - N_EXPERTS)))
    rb = jnp.pad(router_b[0], (0, LANES - N_EXPERTS))
    return [row(norm_mix_g[0]), regroup(w_in[0]).astype(BF16), row(regroup(b_in[0])),
            dww, row(conv_dw_b[0]), row(conv_ln_g[0]), row(conv_ln_b[0]),
            conv_pw_w[0].astype(BF16), row(conv_pw_b[0]), row(mlstm_norm_g[0]),
            mlstm_out_w[0].astype(BF16), w_out[0].astype(BF16), row(norm_ffn_g[0]),
            rw, row(rb)]


def _mixer(x, meta_tokens, weights):
    bsz, seq, d = x.shape
    t = SEQ_TILE
    w_specs = [_const_spec(a.shape) for a in weights]
    state_shapes = [(CONV_HIST, d), (M_HEADS, M_DK, M_DV), (M_HEADS, SUBLANES, M_DK),
                    (M_HEADS, SUBLANES, LANES)]
    cparams = pltpu.CompilerParams(vmem_limit_bytes=VMEM_LIMIT,
                                   dimension_semantics=("arbitrary", "arbitrary"))

    x_meta = jnp.concatenate([jnp.zeros((t - N_META, d), F32), meta_tokens.astype(F32)], axis=0)
    state = pl.pallas_call(
        _prefix_kernel,
        grid=(1,),
        in_specs=[pl.BlockSpec((t, d), lambda i: (0, 0))] + w_specs,
        out_specs=[pl.BlockSpec(s, lambda i, n=len(s): (0,) * n) for s in state_shapes],
        out_shape=[jax.ShapeDtypeStruct(s, F32) for s in state_shapes],
        compiler_params=pltpu.CompilerParams(vmem_limit_bytes=VMEM_LIMIT),
        name="prefix",
    )(x_meta, *weights)

    tile = lambda b, s: (b, s, 0)
    outs = pl.pallas_call(
        _mixer_kernel,
        grid=(bsz, seq // t),
        in_specs=([pl.BlockSpec((1, t, d), tile)] + w_specs
                  + [_const_spec(s) for s in state_shapes]),
        out_specs=[pl.BlockSpec((1, t, d), tile), pl.BlockSpec((1, t, d), tile),
                   pl.BlockSpec((1, t, LANES), tile), pl.BlockSpec((1, t, LANES), tile)],
        out_shape=[jax.ShapeDtypeStruct((bsz, seq, d), F32),
                   jax.ShapeDtypeStruct((bsz, seq, d), F32),
                   jax.ShapeDtypeStruct((bsz, seq, LANES), jnp.int32),
                   jax.ShapeDtypeStruct((bsz, seq, LANES), F32)],
        scratch_shapes=[pltpu.VMEM((CONV_HIST + t, d), F32),
                        pltpu.VMEM((M_HEADS, M_DK, M_DV), F32),
                        pltpu.VMEM((M_HEADS, SUBLANES, M_DK), F32),
                        pltpu.VMEM((M_HEADS, SUBLANES, LANES), F32)],
        compiler_params=cparams,
        name="mixer",
    )(x, *weights, *state)
    return outs


def _row_copy(src_hbm, row, dst, sem):
    return pltpu.make_async_copy(src_hbm.at[pl.ds(row, 1), :], dst, sem)


def _expert_kernel(cnt_ref, start_ref, total_ref, slot_tok_ref,
                   x_hbm, w1_ref, b1_ref, w2_ref, b2_ref, ys_hbm,
                   w1b, w2b, xbuf, ybuf, gsem, osem):
    e = pl.program_id(0)
    nb = cnt_ref[e]
    b0 = start_ref[e]
    total = total_ref[0]
    bm = MOE_BM

    def gather_start(blk):
        slot = blk % 2

        def body(r, carry):
            tok = slot_tok_ref[blk * bm + r]
            _row_copy(x_hbm, tok, xbuf.at[slot, pl.ds(r, 1), :], gsem.at[slot]).start()
            return carry

        lax.fori_loop(0, bm, body, 0)

    def gather_wait(slot):
        pltpu.make_async_copy(x_hbm.at[pl.ds(0, bm), :], xbuf.at[slot], gsem.at[slot]).wait()

    def out_copy(blk, slot):
        return pltpu.make_async_copy(ybuf.at[slot], ys_hbm.at[pl.ds(blk * bm, bm), :], osem.at[slot])

    @pl.when(nb > 0)
    def _():
        w1b[...] = w1_ref[0].astype(BF16)
        w2b[...] = w2_ref[0].astype(BF16)

        @pl.when(b0 == 0)
        def _():
            gather_start(0)

        def block(i, carry):
            blk = b0 + i
            slot = blk % 2
            gather_wait(slot)

            @pl.when(blk + 1 < total)
            def _():
                gather_start(blk + 1)

            xb = xbuf[slot].astype(BF16)
            hcat = jnp.dot(xb, w1b[...], preferred_element_type=F32) + b1_ref[0]
            h_glu = jnp.minimum(hcat[:, :D_FF], SWIGLU_LIMIT)
            h_lin = jnp.clip(hcat[:, D_FF:], -SWIGLU_LIMIT, SWIGLU_LIMIT)
            act = h_glu * jax.nn.sigmoid(SWIGLU_ALPHA * h_glu) * (h_lin + 1.0)
            y = jnp.dot(act.astype(BF16), w2b[...], preferred_element_type=F32) + b2_ref[0]

            @pl.when(blk >= 2)
            def _():
                out_copy(blk - 2, slot).wait()

            ybuf[slot] = y
            out_copy(blk, slot).start()
            return carry

        lax.fori_loop(0, nb, block, 0)

    @pl.when(e == pl.num_programs(0) - 1)
    def _():
        @pl.when(total >= 2)
        def _():
            out_copy(total - 2, (total - 2) % 2).wait()

        @pl.when(total >= 1)
        def _():
            out_copy(total - 1, (total - 1) % 2).wait()

        n_blocks = ys_hbm.shape[0] // bm
        ybuf[0] = jnp.zeros((bm, D_MODEL), F32)

        def zero_start(blk, carry):
            out_copy(blk, 0).start()
            return carry

        def zero_wait(blk, carry):
            out_copy(blk, 0).wait()
            return carry

        lax.fori_loop(total, n_blocks, zero_start, 0)
        lax.fori_loop(total, n_blocks, zero_wait, 0)


def _experts(u2, blk_cnt, blk_start, blk_total, slot_tok, w1, b1, w2, b2):
    n_slots = slot_tok.shape[0]
    d = D_MODEL
    emap = lambda e, *_: (e, 0, 0)
    grid_spec = pltpu.PrefetchScalarGridSpec(
        num_scalar_prefetch=4,
        grid=(N_EXPERTS,),
        in_specs=[pl.BlockSpec(memory_space=pl.ANY),
                  pl.BlockSpec((1, d, 2 * D_FF), emap),
                  pl.BlockSpec((1, 1, 2 * D_FF), emap),
                  pl.BlockSpec((1, D_FF, d), emap),
                  pl.BlockSpec((1, 1, d), emap)],
        out_specs=pl.BlockSpec(memory_space=pl.ANY),
        scratch_shapes=[pltpu.VMEM((d, 2 * D_FF), BF16),
                        pltpu.VMEM((D_FF, d), BF16),
                        pltpu.VMEM((2, MOE_BM, d), F32),
                        pltpu.VMEM((2, MOE_BM, d), F32),
                        pltpu.SemaphoreType.DMA((2,)),
                        pltpu.SemaphoreType.DMA((2,))],
    )
    return pl.pallas_call(
        _expert_kernel,
        grid_spec=grid_spec,
        out_shape=jax.ShapeDtypeStruct((n_slots, d), F32),
        compiler_params=pltpu.CompilerParams(vmem_limit_bytes=VMEM_LIMIT,
                                             dimension_semantics=("arbitrary",)),
        name="experts",
    )(blk_cnt, blk_start, blk_total, slot_tok, u2,
      w1, b1.reshape(N_EXPERTS, 1, 2 * D_FF), w2, b2.reshape(N_EXPERTS, 1, d))


def _combine_kernel(dest_ref, h1_ref, gate_ref, g_ref, ys_hbm, out_ref, gbuf, gsem):
    i = pl.program_id(0)
    nt = pl.num_programs(0)
    tt = COMB_TILE

    def gather_start(tile):
        slot = tile % 2

        def body(r, carry):
            for kk in range(TOP_K):
                row = dest_ref[(tile * tt + r) * TOP_K + kk]
                _row_copy(ys_hbm, row, gbuf.at[slot, kk, pl.ds(r, 1), :], gsem.at[slot]).start()
            return carry

        lax.fori_loop(0, tt, body, 0)

    @pl.when(i == 0)
    def _():
        gather_start(0)

    @pl.when(i + 1 < nt)
    def _():
        gather_start(i + 1)

    slot = i % 2
    for kk in range(TOP_K):
        pltpu.make_async_copy(ys_hbm.at[pl.ds(0, tt), :], gbuf.at[slot, kk], gsem.at[slot]).wait()
    acc = h1_ref[...]
    gates = gate_ref[...]
    for kk in range(TOP_K):
        acc = acc + gates[:, kk:kk + 1] * gbuf[slot, kk]
    out_ref[...] = _rms_norm(acc, g_ref[...])


def _combine(dest, h1, gates, final_g, ys):
    n, d = h1.shape
    tt = COMB_TILE
    grid_spec = pltpu.PrefetchScalarGridSpec(
        num_scalar_prefetch=1,
        grid=(n // tt,),
        in_specs=[pl.BlockSpec((tt, d), lambda i, *_: (i, 0)),
                  pl.BlockSpec((tt, LANES), lambda i, *_: (i, 0)),
                  pl.BlockSpec((1, d), lambda i, *_: (0, 0)),
                  pl.BlockSpec(memory_space=pl.ANY)],
        out_specs=pl.BlockSpec((tt, d), lambda i, *_: (i, 0)),
        scratch_shapes=[pltpu.VMEM((2, TOP_K, tt, d), F32),
                        pltpu.SemaphoreType.DMA((2,))],
    )
    return pl.pallas_call(
        _combine_kernel,
        grid_spec=grid_spec,
        out_shape=jax.ShapeDtypeStruct((n, d), F32),
        compiler_params=pltpu.CompilerParams(vmem_limit_bytes=VMEM_LIMIT,
                                             dimension_semantics=("arbitrary",)),
        name="combine",
    )(dest, h1, gates, final_g.reshape(1, d).astype(F32), ys)


def _routing_tables(top_idx):
    n_tok = top_idx.shape[0]
    n_asg = n_tok * TOP_K
    bm = MOE_BM
    flat_e = top_idx.reshape(-1)
    order = jnp.argsort(flat_e).astype(jnp.int32)
    sorted_e = flat_e[order]
    counts = jnp.bincount(flat_e, length=N_EXPERTS).astype(jnp.int32)
    blk_cnt = (counts + bm - 1) // bm
    blk_end = jnp.cumsum(blk_cnt).astype(jnp.int32)
    blk_start = blk_end - blk_cnt
    grp_start = jnp.cumsum(counts).astype(jnp.int32) - counts
    dest_sorted = (blk_start[sorted_e] * bm + jnp.arange(n_asg, dtype=jnp.int32)
                   - grp_start[sorted_e])
    n_slots = n_asg + N_EXPERTS * bm
    slot_tok = jnp.zeros((n_slots,), jnp.int32).at[dest_sorted].set(order // TOP_K)
    dest = jnp.zeros((n_asg,), jnp.int32).at[order].set(dest_sorted)
    return blk_cnt, blk_start, blk_end[-1:], slot_tok, dest


def kernel(x, meta_tokens, norm_mix_g, w_in, b_in, conv_dw_w, conv_dw_b, conv_ln_g, conv_ln_b, conv_pw_w, conv_pw_b, mlstm_norm_g, mlstm_out_w, w_out, norm_ffn_g, router_w, router_b, expert_w1, expert_b1, expert_w2, expert_b2, final_norm_g):
    bsz, seq, d = x.shape
    assert d == D_MODEL and seq % SEQ_TILE == 0 and w_in.shape[0] == 1
    weights = _mixer_weights(norm_mix_g, w_in, b_in, conv_dw_w, conv_dw_b, conv_ln_g, conv_ln_b,
                             conv_pw_w, conv_pw_b, mlstm_norm_g, mlstm_out_w, w_out, norm_ffn_g,
                             router_w, router_b)
    h1, u2, idx, gates = _mixer(x.astype(F32), meta_tokens, weights)
    n_tok = bsz * seq
    h1 = h1.reshape(n_tok, d)
    u2 = u2.reshape(n_tok, d)
    gates = gates.reshape(n_tok, LANES)
    top_idx = idx.reshape(n_tok, LANES)[:, :TOP_K]
    blk_cnt, blk_start, blk_total, slot_tok, dest = _routing_tables(top_idx)
    ys = _experts(u2, blk_cnt, blk_start, blk_total, slot_tok,
                  expert_w1[0], expert_b1[0], expert_w2[0], expert_b2[0])
    out = _combine(dest, h1, gates, final_norm_g, ys)
    return out.reshape(bsz, seq, d)
```

```python
import jax
import jax.numpy as jnp
from jax import lax
from jax.experimental import pallas as pl
from jax.experimental.pallas import tpu as pltpu

F32 = jnp.float32
BF16 = jnp.bfloat16
U32 = jnp.uint32

D_MODEL = 1024
N_META = 16
CONV_WIDTH = 31
M_HEADS = 4
M_DK = 128
M_DV = 256
QK_SCALE = M_DK ** -0.5
N_EXPERTS = 32
TOP_K = 4
D_FF = D_MODEL
SWIGLU_ALPHA = 1.702
SWIGLU_LIMIT = 7.0
NORM_EPS = 1e-5
NEG_GATE = -1.0e4

LANES = 128
SUBLANES = 8
VMEM_LIMIT = 56 * 1024 * 1024

SEQ_TILE = 256
CONV_HIST = 32
CONV_ROWS = 64
PROJ_CHUNK = 256
MOE_BM = 128
COMB_TILE = 512
PACK = 2
PACK_ROWS = D_MODEL // (PACK * LANES)

C_GLU = 0
C_Q = C_GLU + 2 * D_MODEL
C_K = C_Q + M_HEADS * M_DK
C_V = C_K + M_HEADS * M_DK
C_O = C_V + M_HEADS * M_DV
C_GC = C_O + D_MODEL
C_GM = C_GC + D_MODEL
C_IF = C_GM + D_MODEL
N_PROJ = C_IF + LANES


def _rms_norm(x, g):
    return x * lax.rsqrt(jnp.mean(x * x, axis=-1, keepdims=True) + NORM_EPS) * g


def _split_bf16(x, terms):
    parts = []
    for _ in range(terms - 1):
        p = x.astype(BF16)
        parts.append(p)
        x = x - p.astype(F32)
    parts.append(x.astype(BF16))
    return parts


def _proj(u_bf, w, lo, hi):
    return jnp.dot(u_bf, w["win"][:, lo:hi], preferred_element_type=F32) + w["bin"][:, lo:hi]


def _glu(u_bf, w, row_valid):
    zg = _proj(u_bf, w, C_GLU, C_Q)
    y = zg[:, :D_MODEL] * jax.nn.sigmoid(zg[:, D_MODEL:])
    if row_valid is not None:
        y = jnp.where(row_valid, y, 0.0)
    return y


def _conv_block(ybuf, dww_ref, dwb_ref, c, j):
    first = CONV_HIST - (CONV_WIDTH - 1)
    cs = slice(j * LANES, (j + 1) * LANES)
    win = ybuf[c * CONV_ROWS:c * CONV_ROWS + CONV_ROWS + CONV_HIST, cs]
    acc = jnp.broadcast_to(dwb_ref[:, cs], (CONV_ROWS, LANES))
    for phase in range(SUBLANES):
        taps = [k for k in range(CONV_WIDTH) if (-(first + k)) % SUBLANES == phase]
        if not taps:
            continue
        wb = win if phase == 0 else pltpu.roll(win, phase, axis=0)
        for k in taps:
            i0 = first + k + phase
            acc = acc + dww_ref[k:k + 1, cs] * wb[i0:i0 + CONV_ROWS, :]
    return acc


def _mlstm(zcol, w, c_st, n_st, m_st, row_valid, want_h):
    t = SEQ_TILE
    zif = zcol(C_IF, N_PROJ)
    ig_all = zif
    lf_all = jax.nn.log_sigmoid(zif)
    if row_valid is not None:
        ig_all = jnp.where(row_valid, ig_all, NEG_GATE)
        lf_all = jnp.where(row_valid, lf_all, 0.0)
    rows = lax.broadcasted_iota(jnp.int32, (t, t), 0)
    cols = lax.broadcasted_iota(jnp.int32, (t, t), 1)
    causal = cols <= rows
    lf_terms = jnp.concatenate(_split_bf16(lf_all, 3), axis=1)
    b3 = jnp.dot(causal.astype(BF16), lf_terms, preferred_element_type=F32)
    b_all = b3[:, 0:LANES] + b3[:, LANES:2 * LANES] + b3[:, 2 * LANES:3 * LANES]
    b_sh = pltpu.roll(b_all, LANES - M_HEADS, axis=1)
    a_all = ig_all - b_sh
    a_rows = a_all.T
    hs = []
    for h in range(M_HEADS):
        a_row = a_rows[h:h + 1, :]
        a_col = a_all[:, h:h + 1]
        b_col = b_sh[:, h:h + 1]
        m_prev = m_st[h][0:1, 0:1]
        n_prev = n_st[h][0:1, :]
        ct = c_st[h]
        k = zcol(C_K + h * M_DK, C_K + (h + 1) * M_DK)
        v_bf = zcol(C_V + h * M_DV, C_V + (h + 1) * M_DV).astype(BF16)
        amat = jnp.where(causal, a_row, -jnp.inf)
        m_run = jnp.maximum(jnp.max(amat, axis=-1, keepdims=True), m_prev)
        if want_h:
            q = zcol(C_Q + h * M_DK, C_Q + (h + 1) * M_DK) * QK_SCALE
            q_bf = q.astype(BF16)
            s = lax.dot_general(q_bf, k.astype(BF16), (((1,), (1,)), ((), ())),
                                preferred_element_type=F32)
            p = s * jnp.exp(amat - m_run)
            inter_sc = jnp.exp(m_prev - m_run)
            num = (jnp.dot(p.astype(BF16), v_bf, preferred_element_type=F32)
                   + inter_sc * jnp.dot(q_bf, ct.astype(BF16), preferred_element_type=F32))
            den = (jnp.sum(p, axis=-1, keepdims=True)
                   + inter_sc * jnp.sum(q * n_prev, axis=-1, keepdims=True))
            scale = 1.0 / jnp.maximum(jnp.abs(den), jnp.exp(-(b_col + m_run)))
            hh = num * scale
            hh = hh * lax.rsqrt(jnp.mean(hh * hh, axis=-1, keepdims=True) + NORM_EPS)
            hs.append(hh * w["mng"][:, h * M_DV:(h + 1) * M_DV])
        m_last = m_run[t - 1:t, :]
        decay = jnp.exp(m_prev - m_last)
        kw = k * jnp.exp(a_col - m_last)
        c_st[h] = decay * ct + jnp.dot(kw.T.astype(BF16), v_bf, preferred_element_type=F32)
        n_new = decay * n_prev + jnp.sum(kw, axis=0, keepdims=True)
        n_st[h] = jnp.broadcast_to(n_new, (SUBLANES, M_DK))
        m_st[h] = jnp.broadcast_to(b_col[t - 1:t, :] + m_last, (SUBLANES, LANES))
    return jnp.concatenate(hs, axis=1) if want_h else None


def _router(u2, rw_ref, rb_ref):
    t = u2.shape[0]
    u_hi, u_mid = _split_bf16(u2, 2)
    lhs = jnp.concatenate([u_hi, u_mid, u_hi], axis=1)
    logits = jnp.dot(lhs, rw_ref[...], preferred_element_type=F32) + rb_ref[...]
    lane = lax.broadcasted_iota(jnp.int32, (t, LANES), 1)
    cur = jnp.where(lane < N_EXPERTS, logits, -jnp.inf)
    idx_out = jnp.zeros((t, LANES), jnp.int32)
    val_out = jnp.zeros((t, LANES), F32)
    top0 = None
    esum = None
    for kk in range(TOP_K):
        mx = jnp.max(cur, axis=-1, keepdims=True)
        ix = jnp.min(jnp.where(cur == mx, lane, LANES), axis=-1, keepdims=True)
        if kk == 0:
            top0 = mx
        ek = jnp.exp(mx - top0)
        esum = ek if kk == 0 else esum + ek
        idx_out = jnp.where(lane == kk, ix, idx_out)
        val_out = jnp.where(lane == kk, ek, val_out)
        cur = jnp.where(lane == ix, -jnp.inf, cur)
    return idx_out, val_out / esum


def _pack_rows(u2, out_ref):
    t = u2.shape[0]
    bits = lax.bitcast_convert_type(u2.astype(BF16).astype(F32), U32)
    for i in range(PACK_ROWS):
        lo = bits[:, i * 2 * LANES:i * 2 * LANES + LANES]
        hi = bits[:, i * 2 * LANES + LANES:(i + 1) * 2 * LANES]
        out_ref[pl.ds(i, t, stride=PACK_ROWS), :] = hi | (lo >> 16)


def _unpack_rows(tile_ref, slot, rows):
    parts = []
    for i in range(PACK_ROWS):
        word = tile_ref.at[slot][pl.ds(i, rows, stride=PACK_ROWS), :]
        lo = lax.bitcast_convert_type(word << 16, F32)
        hi = lax.bitcast_convert_type(word & jnp.uint32(0xFFFF0000), F32)
        parts += [lo.astype(BF16), hi.astype(BF16)]
    return jnp.concatenate(parts, axis=1)


def _weights(refs):
    names = ("g1", "win", "bin", "dww", "dwb", "lng", "lnb", "pww", "pwb", "mng", "mow",
             "wout", "g2", "rw", "rb")
    return dict(zip(names, refs))


N_WEIGHTS = 15


def _prefix_kernel(*refs):
    x_ref = refs[0]
    w = _weights(refs[1:1 + N_WEIGHTS])
    ytail_ref, c_out, n_out, m_out = refs[1 + N_WEIGHTS:]
    t = SEQ_TILE
    row_valid = lax.broadcasted_iota(jnp.int32, (t, 1), 0) >= t - N_META
    c_out[...] = jnp.zeros(c_out.shape, F32)
    n_out[...] = jnp.zeros(n_out.shape, F32)
    m_out[...] = jnp.zeros(m_out.shape, F32)
    u_bf = _rms_norm(x_ref[...], w["g1"][...]).astype(BF16)
    y = _glu(u_bf, w, row_valid)
    ytail_ref[...] = y[t - CONV_HIST:, :]
    _mlstm(lambda lo, hi: _proj(u_bf, w, lo, hi), w, c_out, n_out, m_out, row_valid, want_h=False)


def _mixer_kernel(*refs):
    x_ref = refs[0]
    w = _weights(refs[1:1 + N_WEIGHTS])
    y0_ref, c0_ref, n0_ref, m0_ref = refs[1 + N_WEIGHTS:5 + N_WEIGHTS]
    h1_ref, u2p_ref, idx_ref, gate_ref = refs[5 + N_WEIGHTS:9 + N_WEIGHTS]
    ybuf, cbuf, zbuf, c_st, n_st, m_st = refs[9 + N_WEIGHTS:]
    t = SEQ_TILE

    @pl.when(pl.program_id(1) == 0)
    def _():
        ybuf[0:CONV_HIST, :] = y0_ref[...]
        c_st[...] = c0_ref[...]
        n_st[...] = n0_ref[...]
        m_st[...] = m0_ref[...]

    x = x_ref[0]
    u_bf = _rms_norm(x, w["g1"][...]).astype(BF16)

    ybuf[CONV_HIST:CONV_HIST + t, :] = _glu(u_bf, w, None)
    chunks = [(lo, min(lo + PROJ_CHUNK, N_PROJ)) for lo in range(C_Q, N_PROJ, PROJ_CHUNK)]
    blocks = [(c, j) for c in range(t // CONV_ROWS) for j in range(D_MODEL // LANES)]
    assert len(chunks) <= len(blocks)
    for n, (c, j) in enumerate(blocks):
        cbuf[c * CONV_ROWS:(c + 1) * CONV_ROWS, j * LANES:(j + 1) * LANES] = _conv_block(
            ybuf, w["dww"], w["dwb"], c, j)
        if n < len(chunks):
            lo, hi = chunks[n]
            zbuf[:, lo - C_Q:hi - C_Q] = _proj(u_bf, w, lo, hi)
    ybuf[0:CONV_HIST, :] = ybuf[t:t + CONV_HIST, :]
    zcol = lambda lo, hi: zbuf[:, lo - C_Q:hi - C_Q]

    conv = cbuf[...]
    mu = jnp.mean(conv, axis=-1, keepdims=True)
    cen = conv - mu
    var = jnp.mean(cen * cen, axis=-1, keepdims=True)
    ln = cen * lax.rsqrt(var + NORM_EPS) * w["lng"][...] + w["lnb"][...]
    conv_out = jnp.dot(jax.nn.silu(ln).astype(BF16), w["pww"][...],
                       preferred_element_type=F32) + w["pwb"][...]

    hcat = _mlstm(zcol, w, c_st, n_st, m_st, None, want_h=True)
    o_gate = jax.nn.sigmoid(zcol(C_O, C_GC))
    mlstm_out = jnp.dot((o_gate * hcat).astype(BF16), w["mow"][...], preferred_element_type=F32)

    g_conv = jax.nn.sigmoid(zcol(C_GC, C_GM))
    g_mlstm = jax.nn.sigmoid(zcol(C_GM, C_IF))
    mix = (g_conv * conv_out + g_mlstm * mlstm_out).astype(BF16)
    h1 = x + jnp.dot(mix, w["wout"][...], preferred_element_type=F32)
    h1_ref[0] = h1

    u2 = _rms_norm(h1, w["g2"][...])
    _pack_rows(u2, u2p_ref)
    idx, gates = _router(u2, w["rw"], w["rb"])
    idx_ref[0] = idx
    gate_ref[0] = gates


def _const_spec(shape):
    nd = len(shape)
    return pl.BlockSpec(shape, lambda *_: (0,) * nd, pipeline_mode=pl.Buffered(1))


def _mixer_weights(norm_mix_g, w_in, b_in, conv_dw_w, conv_dw_b, conv_ln_g, conv_ln_b, conv_pw_w,
                   conv_pw_b, mlstm_norm_g, mlstm_out_w, w_out, norm_ffn_g, router_w, router_b):
    d = D_MODEL
    qo, ko, vo = 2 * d, 2 * d + 512, 2 * d + 1024
    io = vo + d
    oo = io + 2 * M_HEADS

    def regroup(a):
        parts = [a[..., 0:qo], a[..., qo:ko], a[..., ko:vo], a[..., vo:io],
                 a[..., oo:oo + d], a[..., oo + d:oo + 2 * d], a[..., oo + 2 * d:oo + 3 * d],
                 a[..., io:oo],
                 jnp.zeros(a.shape[:-1] + (LANES - 2 * M_HEADS,), a.dtype)]
        return jnp.concatenate(parts, axis=-1)

    row = lambda a: a.reshape(1, -1).astype(F32)
    dww = jnp.pad(conv_dw_w[0], ((0, 32 - CONV_WIDTH), (0, 0)))
    rw = jnp.pad(router_w[0].astype(F32), ((0, 0), (0, LANES - N_EXPERTS)))
    rw_hi = rw.astype(BF16)
    rw_mid = (rw - rw_hi.astype(F32)).astype(BF16)
    rb = jnp.pad(router_b[0], (0, LANES - N_EXPERTS))
    return [row(norm_mix_g[0]), regroup(w_in[0]).astype(BF16), row(regroup(b_in[0])),
            dww, row(conv_dw_b[0]), row(conv_ln_g[0]), row(conv_ln_b[0]),
            conv_pw_w[0].astype(BF16), row(conv_pw_b[0]), row(mlstm_norm_g[0]),
            mlstm_out_w[0].astype(BF16), w_out[0].astype(BF16), row(norm_ffn_g[0]),
            jnp.concatenate([rw_hi, rw_hi, rw_mid], axis=0), row(rb)]


def _mixer(x, meta_tokens, weights):
    bsz, seq, d = x.shape
    t = SEQ_TILE
    n_seq = seq // t
    w_specs = [_const_spec(a.shape) for a in weights]
    state_shapes = [(CONV_HIST, d), (M_HEADS, M_DK, M_DV), (M_HEADS, SUBLANES, M_DK),
                    (M_HEADS, SUBLANES, LANES)]

    x_meta = jnp.concatenate([jnp.zeros((t - N_META, d), F32), meta_tokens.astype(F32)], axis=0)
    state = pl.pallas_call(
        _prefix_kernel,
        grid=(1,),
        in_specs=[pl.BlockSpec((t, d), lambda i: (0, 0))] + w_specs,
        out_specs=[pl.BlockSpec(s, lambda i, n=len(s): (0,) * n) for s in state_shapes],
        out_shape=[jax.ShapeDtypeStruct(s, F32) for s in state_shapes],
        compiler_params=pltpu.CompilerParams(vmem_limit_bytes=VMEM_LIMIT),
        name="prefix",
    )(x_meta, *weights)

    tile = lambda b, s: (b, s, 0)
    return pl.pallas_call(
        _mixer_kernel,
        grid=(bsz, n_seq),
        in_specs=([pl.BlockSpec((1, t, d), tile)] + w_specs
                  + [_const_spec(s) for s in state_shapes]),
        out_specs=[pl.BlockSpec((1, t, d), tile),
                   pl.BlockSpec((t * PACK_ROWS, LANES), lambda b, s: (b * n_seq + s, 0)),
                   pl.BlockSpec((1, t, LANES), tile), pl.BlockSpec((1, t, LANES), tile)],
        out_shape=[jax.ShapeDtypeStruct((bsz, seq, d), F32),
                   jax.ShapeDtypeStruct((bsz * seq * PACK_ROWS, LANES), U32),
                   jax.ShapeDtypeStruct((bsz, seq, LANES), jnp.int32),
                   jax.ShapeDtypeStruct((bsz, seq, LANES), F32)],
        scratch_shapes=[pltpu.VMEM((CONV_HIST + t, d), F32),
                        pltpu.VMEM((t, d), F32),
                        pltpu.VMEM((t, N_PROJ - C_Q), F32),
                        pltpu.VMEM((M_HEADS, M_DK, M_DV), F32),
                        pltpu.VMEM((M_HEADS, SUBLANES, M_DK), F32),
                        pltpu.VMEM((M_HEADS, SUBLANES, LANES), F32)],
        compiler_params=pltpu.CompilerParams(vmem_limit_bytes=VMEM_LIMIT,
                                             dimension_semantics=("arbitrary", "arbitrary")),
        name="mixer",
    )(x, *weights, *state)


def _expert_kernel(cnt_ref, start_ref, total_ref, tok_ref, row_ref,
                   x_ref, w1_ref, b1_ref, w2_ref, b2_ref, ys_hbm,
                   tile, ybuf, osem):
    e = pl.program_id(0)
    nb = cnt_ref[e]
    b0 = start_ref[e]
    total = total_ref[0]
    bm = MOE_BM
    n_planes_rows = ys_hbm.shape[0] - (N_EXPERTS + 1) * bm
    spare_block = n_planes_rows + N_EXPERTS * bm

    def gather(blk, slot):
        for r in range(bm):
            src = pl.multiple_of(tok_ref[blk * bm + r], PACK_ROWS)
            tile[slot, pl.ds(PACK_ROWS * r, PACK_ROWS), :] = x_ref[pl.ds(src, PACK_ROWS), :]

    def scatter_start(blk, slot):
        for r in range(bm):
            dst = row_ref[blk * bm + r]
            pltpu.make_async_copy(ybuf.at[slot, pl.ds(r, 1), :], ys_hbm.at[pl.ds(dst, 1), :],
                                  osem.at[slot]).start()

    def block_copy(slot, row0):
        return pltpu.make_async_copy(ybuf.at[slot], ys_hbm.at[pl.ds(row0, bm), :], osem.at[slot])

    def scatter_wait(slot):
        block_copy(slot, 0).wait()

    def mlp(slot):
        xb = _unpack_rows(tile, slot, bm)
        hcat = jnp.dot(xb, w1_ref[0], preferred_element_type=F32) + b1_ref[0]
        h_glu = jnp.minimum(hcat[:, :D_FF], SWIGLU_LIMIT)
        h_lin = jnp.clip(hcat[:, D_FF:], -SWIGLU_LIMIT, SWIGLU_LIMIT)
        act = h_glu * jax.nn.sigmoid(SWIGLU_ALPHA * h_glu) * (h_lin + 1.0)
        return jnp.dot(act.astype(BF16), w2_ref[0], preferred_element_type=F32) + b2_ref[0]

    @pl.when(e == 0)
    def _():
        ybuf[...] = jnp.zeros(ybuf.shape, F32)
        for i in range(N_EXPERTS):
            block_copy(0, n_planes_rows + i * bm).start()
        for i in range(N_EXPERTS):
            block_copy(0, n_planes_rows + i * bm).wait()

    @pl.when(nb > 0)
    def _():
        @pl.when(b0 == 0)
        def _():
            gather(0, 0)
            y = mlp(0)
            gather(1, 1)
            ybuf[0] = y
            block_copy(1, spare_block).start()

        def block(i, carry):
            blk = b0 + i
            slot = blk % 2
            y = mlp(slot)
            gather(blk + 1, 1 - slot)
            scatter_start(blk - 1, 1 - slot)
            scatter_wait(slot)
            ybuf[slot] = y
            return carry

        lax.fori_loop(jnp.where(b0 == 0, 1, 0), nb, block, 0)

    @pl.when(e == pl.num_programs(0) - 1)
    def _():
        last = (total - 1) % 2
        scatter_start(total - 1, last)
        scatter_wait(1 - last)
        scatter_wait(last)


def _experts(u2p, blk_cnt, blk_start, blk_total, slot_tok, slot_row, n_tok, w1, b1, w2, b2):
    d = D_MODEL
    emap = lambda e, *_: (e, 0, 0)
    n_rows = TOP_K * n_tok + (N_EXPERTS + 1) * MOE_BM
    grid_spec = pltpu.PrefetchScalarGridSpec(
        num_scalar_prefetch=5,
        grid=(N_EXPERTS,),
        in_specs=[pl.BlockSpec(u2p.shape, lambda e, *_: (0, 0), pipeline_mode=pl.Buffered(1)),
                  pl.BlockSpec((1, d, 2 * D_FF), emap),
                  pl.BlockSpec((1, 1, 2 * D_FF), emap),
                  pl.BlockSpec((1, D_FF, d), emap),
                  pl.BlockSpec((1, 1, d), emap)],
        out_specs=pl.BlockSpec(memory_space=pl.ANY),
        scratch_shapes=[pltpu.VMEM((2, MOE_BM * PACK_ROWS, LANES), U32),
                        pltpu.VMEM((2, MOE_BM, d), F32),
                        pltpu.SemaphoreType.DMA((2,))],
    )
    return pl.pallas_call(
        _expert_kernel,
        grid_spec=grid_spec,
        out_shape=jax.ShapeDtypeStruct((n_rows, d), F32),
        compiler_params=pltpu.CompilerParams(vmem_limit_bytes=VMEM_LIMIT,
                                             dimension_semantics=("arbitrary",)),
        name="experts",
    )(blk_cnt, blk_start, blk_total, slot_tok, slot_row, u2p,
      w1.astype(BF16), b1.reshape(N_EXPERTS, 1, 2 * D_FF), w2.astype(BF16),
      b2.reshape(N_EXPERTS, 1, d))


def _combine_kernel(h1_ref, gate_ref, g_ref, y0_ref, y1_ref, y2_ref, y3_ref, out_ref):
    acc = h1_ref[...]
    gates = gate_ref[...]
    for kk, y_ref in enumerate((y0_ref, y1_ref, y2_ref, y3_ref)):
        acc = acc + gates[:, kk:kk + 1] * y_ref[...]
    out_ref[...] = _rms_norm(acc, g_ref[...])


def _combine(h1, gates, final_g, ys):
    n, d = h1.shape
    tt = COMB_TILE
    nt = n // tt
    plane = lambda kk: pl.BlockSpec((tt, d), lambda i, kk=kk: (kk * nt + i, 0))
    return pl.pallas_call(
        _combine_kernel,
        grid=(nt,),
        in_specs=[pl.BlockSpec((tt, d), lambda i: (i, 0)),
                  pl.BlockSpec((tt, LANES), lambda i: (i, 0)),
                  pl.BlockSpec((1, d), lambda i: (0, 0))] + [plane(kk) for kk in range(TOP_K)],
        out_specs=pl.BlockSpec((tt, d), lambda i: (i, 0)),
        out_shape=jax.ShapeDtypeStruct((n, d), F32),
        compiler_params=pltpu.CompilerParams(vmem_limit_bytes=VMEM_LIMIT,
                                             dimension_semantics=("arbitrary",)),
        name="combine",
    )(h1, gates, final_g.reshape(1, d).astype(F32), ys, ys, ys, ys)


def _routing_tables(top_idx):
    n_tok = top_idx.shape[0]
    n_asg = n_tok * TOP_K
    bm = MOE_BM
    flat_e = top_idx.reshape(-1).astype(jnp.int32)
    sorted_e, order = lax.sort((flat_e, jnp.arange(n_asg, dtype=jnp.int32)), num_keys=1)
    experts = jnp.arange(N_EXPERTS, dtype=jnp.int32)
    grp_start = jnp.searchsorted(sorted_e, experts, side="left").astype(jnp.int32)
    grp_end = jnp.searchsorted(sorted_e, experts, side="right").astype(jnp.int32)
    counts = grp_end - grp_start
    blk_cnt = (counts + bm - 1) // bm
    blk_end = jnp.cumsum(blk_cnt).astype(jnp.int32)
    blk_start = blk_end - blk_cnt
    n_blocks = n_asg // bm + N_EXPERTS
    blk = jnp.arange(n_blocks, dtype=jnp.int32)
    blk_e = jnp.minimum(jnp.searchsorted(blk_end, blk, side="right"),
                        N_EXPERTS - 1).astype(jnp.int32)
    e_count = counts[blk_e][:, None]
    e_first = grp_start[blk_e][:, None]
    within = ((blk - blk_start[blk_e]) * bm)[:, None] + jnp.arange(bm, dtype=jnp.int32)[None, :]
    real = within < e_count
    asg = order[jnp.where(real, e_first + within, 0)]
    tok = asg // TOP_K
    pad_ord = blk[:, None] * bm + jnp.arange(bm, dtype=jnp.int32)[None, :] - (
        e_first + jnp.minimum(within, e_count))
    slot_tok = jnp.where(real, tok * PACK_ROWS, 0).astype(jnp.int32).reshape(-1)
    slot_row = jnp.where(real, (asg % TOP_K) * n_tok + tok,
                         n_asg + pad_ord).astype(jnp.int32).reshape(-1)
    return blk_cnt, blk_start, blk_end[-1:], slot_tok, slot_row


def kernel(x, meta_tokens, norm_mix_g, w_in, b_in, conv_dw_w, conv_dw_b, conv_ln_g, conv_ln_b, conv_pw_w, conv_pw_b, mlstm_norm_g, mlstm_out_w, w_out, norm_ffn_g, router_w, router_b, expert_w1, expert_b1, expert_w2, expert_b2, final_norm_g):
    bsz, seq, d = x.shape
    n_tok = bsz * seq
    assert d == D_MODEL and seq % SEQ_TILE == 0 and n_tok % COMB_TILE == 0 and w_in.shape[0] == 1
    weights = _mixer_weights(norm_mix_g, w_in, b_in, conv_dw_w, conv_dw_b, conv_ln_g, conv_ln_b,
                             conv_pw_w, conv_pw_b, mlstm_norm_g, mlstm_out_w, w_out, norm_ffn_g,
                             router_w, router_b)
    h1, u2p, idx, gates = _mixer(x.astype(F32), meta_tokens, weights)
    h1 = h1.reshape(n_tok, d)
    gates = gates.reshape(n_tok, LANES)
    top_idx = idx.reshape(n_tok, LANES)[:, :TOP_K]
    blk_cnt, blk_start, blk_total, slot_tok, slot_row = _routing_tables(top_idx)
    ys = _experts(u2p, blk_cnt, blk_start, blk_total, slot_tok, slot_row, n_tok,
                  expert_w1[0], expert_b1[0], expert_w2[0], expert_b2[0])
    out = _combine(h1, gates, final_norm_g, ys)
    return out.reshape(bsz, seq, d)
```

```python
import jax
import jax.numpy as jnp
from jax import lax
from jax.experimental import pallas as pl
from jax.experimental.pallas import tpu as pltpu

F32 = jnp.float32
BF16 = jnp.bfloat16
U32 = jnp.uint32

D_MODEL = 1024
N_META = 16
CONV_WIDTH = 31
M_HEADS = 4
M_DK = 128
M_DV = 256
QK_SCALE = M_DK ** -0.5
N_EXPERTS = 32
TOP_K = 4
D_FF = D_MODEL
SWIGLU_ALPHA = 1.702
SWIGLU_LIMIT = 7.0
NORM_EPS = 1e-5
NEG_GATE = -1.0e4

LANES = 128
SUBLANES = 8
VMEM_LIMIT = 56 * 1024 * 1024

SEQ_TILE = 256
CONV_HIST = 32
CONV_ROWS = 64
PROJ_CHUNK = 256
MOE_BM = 128
W_CHUNK = 128
COMB_TILE = 512
PACK = 2
PACK_ROWS = D_MODEL // (PACK * LANES)

C_GLU = 0
C_Q = C_GLU + 2 * D_MODEL
C_K = C_Q + M_HEADS * M_DK
C_V = C_K + M_HEADS * M_DK
C_O = C_V + M_HEADS * M_DV
C_GC = C_O + D_MODEL
C_GM = C_GC + D_MODEL
C_IF = C_GM + D_MODEL
N_PROJ = C_IF + LANES


def _rms_norm(x, g):
    return x * lax.rsqrt(jnp.mean(x * x, axis=-1, keepdims=True) + NORM_EPS) * g


def _split_bf16(x, terms):
    parts = []
    for _ in range(terms - 1):
        p = x.astype(BF16)
        parts.append(p)
        x = x - p.astype(F32)
    parts.append(x.astype(BF16))
    return parts


def _proj(u_bf, w, lo, hi):
    return jnp.dot(u_bf, w["win"][:, lo:hi], preferred_element_type=F32) + w["bin"][:, lo:hi]


def _glu(u_bf, w, row_valid):
    zg = _proj(u_bf, w, C_GLU, C_Q)
    y = zg[:, :D_MODEL] * jax.nn.sigmoid(zg[:, D_MODEL:])
    if row_valid is not None:
        y = jnp.where(row_valid, y, 0.0)
    return y


def _conv_block(ybuf, dww_ref, dwb_ref, c, j):
    first = CONV_HIST - (CONV_WIDTH - 1)
    cs = slice(j * LANES, (j + 1) * LANES)
    n_sub = CONV_ROWS // SUBLANES
    win = ybuf[c * CONV_ROWS:c * CONV_ROWS + CONV_ROWS + CONV_HIST, cs]
    acc = [jnp.broadcast_to(dwb_ref[:, cs], (SUBLANES, LANES))] * n_sub
    for phase in range(SUBLANES):
        taps = [k for k in range(CONV_WIDTH) if (-(first + k)) % SUBLANES == phase]
        if not taps:
            continue
        wb = win if phase == 0 else pltpu.roll(win, phase, axis=0)
        for k in taps:
            i0 = first + k + phase
            wk = dww_ref[k, :, cs]
            acc = [acc[r] + wk * wb[i0 + r * SUBLANES:i0 + (r + 1) * SUBLANES, :]
                   for r in range(n_sub)]
    return jnp.concatenate(acc, axis=0)


def _mlstm(zcol, w, c_st, n_st, m_st, row_valid, want_h):
    t = SEQ_TILE
    zif = zcol(C_IF, N_PROJ)
    ig_all = zif
    lf_all = jax.nn.log_sigmoid(zif)
    if row_valid is not None:
        ig_all = jnp.where(row_valid, ig_all, NEG_GATE)
        lf_all = jnp.where(row_valid, lf_all, 0.0)
    rows = lax.broadcasted_iota(jnp.int32, (t, t), 0)
    cols = lax.broadcasted_iota(jnp.int32, (t, t), 1)
    causal = cols <= rows
    lf_terms = jnp.concatenate(_split_bf16(lf_all, 3), axis=1)
    b3 = jnp.dot(causal.astype(BF16), lf_terms, preferred_element_type=F32)
    b_all = b3[:, 0:LANES] + b3[:, LANES:2 * LANES] + b3[:, 2 * LANES:3 * LANES]
    b_sh = pltpu.roll(b_all, LANES - M_HEADS, axis=1)
    a_all = ig_all - b_sh
    a_rows = a_all.T
    hs = []
    for h in range(M_HEADS):
        a_row = a_rows[h:h + 1, :]
        a_col = a_all[:, h:h + 1]
        b_col = b_sh[:, h:h + 1]
        m_prev = m_st[h][0:1, 0:1]
        n_prev = n_st[h][0:1, :]
        ct = c_st[h]
        k = zcol(C_K + h * M_DK, C_K + (h + 1) * M_DK)
        v_bf = zcol(C_V + h * M_DV, C_V + (h + 1) * M_DV).astype(BF16)
        amat = jnp.where(causal, a_row, -jnp.inf)
        m_run = jnp.maximum(jnp.max(amat, axis=-1, keepdims=True), m_prev)
        if want_h:
            q = zcol(C_Q + h * M_DK, C_Q + (h + 1) * M_DK) * QK_SCALE
            q_bf = q.astype(BF16)
            s = lax.dot_general(q_bf, k.astype(BF16), (((1,), (1,)), ((), ())),
                                preferred_element_type=F32)
            p = s * jnp.exp(amat - m_run)
            inter_sc = jnp.exp(m_prev - m_run)
            num = (jnp.dot(p.astype(BF16), v_bf, preferred_element_type=F32)
                   + inter_sc * jnp.dot(q_bf, ct.astype(BF16), preferred_element_type=F32))
            den = (jnp.sum(p, axis=-1, keepdims=True)
                   + inter_sc * jnp.sum(q * n_prev, axis=-1, keepdims=True))
            scale = 1.0 / jnp.maximum(jnp.abs(den), jnp.exp(-(b_col + m_run)))
            hh = num * scale
            hh = hh * lax.rsqrt(jnp.mean(hh * hh, axis=-1, keepdims=True) + NORM_EPS)
            hs.append(hh * w["mng"][:, h * M_DV:(h + 1) * M_DV])
        m_last = m_run[t - 1:t, :]
        decay = jnp.exp(m_prev - m_last)
        kw = k * jnp.exp(a_col - m_last)
        c_st[h] = decay * ct + jnp.dot(kw.T.astype(BF16), v_bf, preferred_element_type=F32)
        n_new = decay * n_prev + jnp.sum(kw, axis=0, keepdims=True)
        n_st[h] = jnp.broadcast_to(n_new, (SUBLANES, M_DK))
        m_st[h] = jnp.broadcast_to(b_col[t - 1:t, :] + m_last, (SUBLANES, LANES))
    return jnp.concatenate(hs, axis=1) if want_h else None


def _router(u2, rw_ref, rb_ref):
    t = u2.shape[0]
    u_hi, u_mid = _split_bf16(u2, 2)
    lhs = jnp.concatenate([u_hi, u_mid, u_hi], axis=1)
    logits = jnp.dot(lhs, rw_ref[...], preferred_element_type=F32) + rb_ref[...]
    lane = lax.broadcasted_iota(jnp.int32, (t, LANES), 1)
    cur = jnp.where(lane < N_EXPERTS, logits, -jnp.inf)
    idx_out = jnp.zeros((t, LANES), jnp.int32)
    val_out = jnp.zeros((t, LANES), F32)
    top0 = None
    esum = None
    for kk in range(TOP_K):
        mx = jnp.max(cur, axis=-1, keepdims=True)
        ix = jnp.min(jnp.where(cur == mx, lane, LANES), axis=-1, keepdims=True)
        if kk == 0:
            top0 = mx
        ek = jnp.exp(mx - top0)
        esum = ek if kk == 0 else esum + ek
        idx_out = jnp.where(lane == kk, ix, idx_out)
        val_out = jnp.where(lane == kk, ek, val_out)
        cur = jnp.where(lane == ix, -jnp.inf, cur)
    return idx_out, val_out / esum


def _pack_rows(u2, out_ref):
    t = u2.shape[0]
    bits = lax.bitcast_convert_type(u2.astype(BF16).astype(F32), U32)
    for i in range(PACK_ROWS):
        lo = bits[:, i * 2 * LANES:i * 2 * LANES + LANES]
        hi = bits[:, i * 2 * LANES + LANES:(i + 1) * 2 * LANES]
        out_ref[pl.ds(i, t, stride=PACK_ROWS), :] = hi | (lo >> 16)


def _unpack_rows(tile_ref, slot, rows):
    parts = []
    for i in range(PACK_ROWS):
        word = tile_ref.at[slot][pl.ds(i, rows, stride=PACK_ROWS), :]
        lo = lax.bitcast_convert_type(word << 16, F32)
        hi = lax.bitcast_convert_type(word & jnp.uint32(0xFFFF0000), F32)
        parts += [lo.astype(BF16), hi.astype(BF16)]
    return jnp.concatenate(parts, axis=1)


def _weights(refs):
    names = ("g1", "win", "bin", "dww", "dwb", "lng", "lnb", "pww", "pwb", "mng", "mow",
             "wout", "g2", "rw", "rb")
    return dict(zip(names, refs))


N_WEIGHTS = 15


def _prefix_kernel(*refs):
    x_ref = refs[0]
    w = _weights(refs[1:1 + N_WEIGHTS])
    ytail_ref, c_out, n_out, m_out = refs[1 + N_WEIGHTS:]
    t = SEQ_TILE
    row_valid = lax.broadcasted_iota(jnp.int32, (t, 1), 0) >= t - N_META
    c_out[...] = jnp.zeros(c_out.shape, F32)
    n_out[...] = jnp.zeros(n_out.shape, F32)
    m_out[...] = jnp.zeros(m_out.shape, F32)
    u_bf = _rms_norm(x_ref[...], w["g1"][...]).astype(BF16)
    y = _glu(u_bf, w, row_valid)
    ytail_ref[...] = y[t - CONV_HIST:, :]
    _mlstm(lambda lo, hi: _proj(u_bf, w, lo, hi), w, c_out, n_out, m_out, row_valid, want_h=False)


def _mixer_kernel(*refs):
    x_ref = refs[0]
    w = _weights(refs[1:1 + N_WEIGHTS])
    y0_ref, c0_ref, n0_ref, m0_ref = refs[1 + N_WEIGHTS:5 + N_WEIGHTS]
    h1_ref, u2p_ref, idx_ref, gate_ref = refs[5 + N_WEIGHTS:9 + N_WEIGHTS]
    ybuf, cbuf, zbuf, c_st, n_st, m_st = refs[9 + N_WEIGHTS:]
    t = SEQ_TILE

    @pl.when(pl.program_id(1) == 0)
    def _():
        ybuf[0:CONV_HIST, :] = y0_ref[...]
        c_st[...] = c0_ref[...]
        n_st[...] = n0_ref[...]
        m_st[...] = m0_ref[...]

    x = x_ref[0]
    u_bf = _rms_norm(x, w["g1"][...]).astype(BF16)

    ybuf[CONV_HIST:CONV_HIST + t, :] = _glu(u_bf, w, None)
    chunks = [(lo, min(lo + PROJ_CHUNK, N_PROJ)) for lo in range(C_Q, N_PROJ, PROJ_CHUNK)]
    blocks = [(c, j) for c in range(t // CONV_ROWS) for j in range(D_MODEL // LANES)]
    assert len(chunks) <= len(blocks)
    for n, (c, j) in enumerate(blocks):
        cbuf[c * CONV_ROWS:(c + 1) * CONV_ROWS, j * LANES:(j + 1) * LANES] = _conv_block(
            ybuf, w["dww"], w["dwb"], c, j)
        if n < len(chunks):
            lo, hi = chunks[n]
            zbuf[:, lo - C_Q:hi - C_Q] = _proj(u_bf, w, lo, hi)
    ybuf[0:CONV_HIST, :] = ybuf[t:t + CONV_HIST, :]
    zcol = lambda lo, hi: zbuf[:, lo - C_Q:hi - C_Q]

    conv = cbuf[...]
    mu = jnp.mean(conv, axis=-1, keepdims=True)
    cen = conv - mu
    var = jnp.mean(cen * cen, axis=-1, keepdims=True)
    ln = cen * lax.rsqrt(var + NORM_EPS) * w["lng"][...] + w["lnb"][...]
    conv_out = jnp.dot(jax.nn.silu(ln).astype(BF16), w["pww"][...],
                       preferred_element_type=F32) + w["pwb"][...]

    hcat = _mlstm(zcol, w, c_st, n_st, m_st, None, want_h=True)
    o_gate = jax.nn.sigmoid(zcol(C_O, C_GC))
    mlstm_out = jnp.dot((o_gate * hcat).astype(BF16), w["mow"][...], preferred_element_type=F32)

    g_conv = jax.nn.sigmoid(zcol(C_GC, C_GM))
    g_mlstm = jax.nn.sigmoid(zcol(C_GM, C_IF))
    mix = (g_conv * conv_out + g_mlstm * mlstm_out).astype(BF16)
    h1 = x + jnp.dot(mix, w["wout"][...], preferred_element_type=F32)
    h1_ref[0] = h1

    u2 = _rms_norm(h1, w["g2"][...])
    _pack_rows(u2, u2p_ref)
    idx, gates = _router(u2, w["rw"], w["rb"])
    idx_ref[0] = idx
    gate_ref[0] = gates


def _const_spec(shape):
    nd = len(shape)
    return pl.BlockSpec(shape, lambda *_: (0,) * nd, pipeline_mode=pl.Buffered(1))


def _mixer_weights(norm_mix_g, w_in, b_in, conv_dw_w, conv_dw_b, conv_ln_g, conv_ln_b, conv_pw_w,
                   conv_pw_b, mlstm_norm_g, mlstm_out_w, w_out, norm_ffn_g, router_w, router_b):
    d = D_MODEL
    qo, ko, vo = 2 * d, 2 * d + 512, 2 * d + 1024
    io = vo + d
    oo = io + 2 * M_HEADS

    def regroup(a):
        parts = [a[..., 0:qo], a[..., qo:ko], a[..., ko:vo], a[..., vo:io],
                 a[..., oo:oo + d], a[..., oo + d:oo + 2 * d], a[..., oo + 2 * d:oo + 3 * d],
                 a[..., io:oo],
                 jnp.zeros(a.shape[:-1] + (LANES - 2 * M_HEADS,), a.dtype)]
        return jnp.concatenate(parts, axis=-1)

    row = lambda a: a.reshape(1, -1).astype(F32)
    dww = jnp.broadcast_to(conv_dw_w[0].astype(F32)[:, None, :], (CONV_WIDTH, SUBLANES, d))
    rw = jnp.pad(router_w[0].astype(F32), ((0, 0), (0, LANES - N_EXPERTS)))
    rw_hi = rw.astype(BF16)
    rw_mid = (rw - rw_hi.astype(F32)).astype(BF16)
    rb = jnp.pad(router_b[0], (0, LANES - N_EXPERTS))
    return [row(norm_mix_g[0]), regroup(w_in[0].astype(BF16)), row(regroup(b_in[0])),
            dww, row(conv_dw_b[0]), row(conv_ln_g[0]), row(conv_ln_b[0]),
            conv_pw_w[0].astype(BF16), row(conv_pw_b[0]), row(mlstm_norm_g[0]),
            mlstm_out_w[0].astype(BF16), w_out[0].astype(BF16), row(norm_ffn_g[0]),
            jnp.concatenate([rw_hi, rw_hi, rw_mid], axis=0), row(rb)]


def _mixer(x, meta_tokens, weights):
    bsz, seq, d = x.shape
    t = SEQ_TILE
    n_seq = seq // t
    w_specs = [_const_spec(a.shape) for a in weights]
    state_shapes = [(CONV_HIST, d), (M_HEADS, M_DK, M_DV), (M_HEADS, SUBLANES, M_DK),
                    (M_HEADS, SUBLANES, LANES)]

    x_meta = jnp.concatenate([jnp.zeros((t - N_META, d), F32), meta_tokens.astype(F32)], axis=0)
    state = pl.pallas_call(
        _prefix_kernel,
        grid=(1,),
        in_specs=[pl.BlockSpec((t, d), lambda i: (0, 0))] + w_specs,
        out_specs=[pl.BlockSpec(s, lambda i, n=len(s): (0,) * n) for s in state_shapes],
        out_shape=[jax.ShapeDtypeStruct(s, F32) for s in state_shapes],
        compiler_params=pltpu.CompilerParams(vmem_limit_bytes=VMEM_LIMIT),
        name="prefix",
    )(x_meta, *weights)

    tile = lambda b, s: (b, s, 0)
    return pl.pallas_call(
        _mixer_kernel,
        grid=(bsz, n_seq),
        in_specs=([pl.BlockSpec((1, t, d), tile)] + w_specs
                  + [_const_spec(s) for s in state_shapes]),
        out_specs=[pl.BlockSpec((1, t, d), tile),
                   pl.BlockSpec((t * PACK_ROWS, LANES), lambda b, s: (b * n_seq + s, 0)),
                   pl.BlockSpec((1, t, LANES), tile), pl.BlockSpec((1, t, LANES), tile)],
        out_shape=[jax.ShapeDtypeStruct((bsz, seq, d), F32),
                   jax.ShapeDtypeStruct((bsz * seq * PACK_ROWS, LANES), U32),
                   jax.ShapeDtypeStruct((bsz, seq, LANES), jnp.int32),
                   jax.ShapeDtypeStruct((bsz, seq, LANES), F32)],
        scratch_shapes=[pltpu.VMEM((CONV_HIST + t, d), F32),
                        pltpu.VMEM((t, d), F32),
                        pltpu.VMEM((t, N_PROJ - C_Q), F32),
                        pltpu.VMEM((M_HEADS, M_DK, M_DV), F32),
                        pltpu.VMEM((M_HEADS, SUBLANES, M_DK), F32),
                        pltpu.VMEM((M_HEADS, SUBLANES, LANES), F32)],
        compiler_params=pltpu.CompilerParams(vmem_limit_bytes=VMEM_LIMIT,
                                             dimension_semantics=("arbitrary", "arbitrary")),
        name="mixer",
    )(x, *weights, *state)


def _chunk_rows(c):
    start = c * W_CHUNK
    return pl.ds(start if isinstance(c, int) else pl.multiple_of(start, W_CHUNK), W_CHUNK)


def _expert_kernel(cnt_ref, start_ref, total_ref, tok_ref, row_ref,
                   x_ref, w1_hbm, b1_ref, w2_hbm, b2_ref, ys_hbm,
                   tile, ybuf, wb1, wb2, st1, st2, osem, wsem1, wsem2):
    e = pl.program_id(0)
    n_exp = pl.num_programs(0)
    nb = cnt_ref[e]
    b0 = start_ref[e]
    total = total_ref[0]
    bm = MOE_BM
    n_planes_rows = ys_hbm.shape[0] - (N_EXPERTS + 1) * bm
    spare_block = n_planes_rows + N_EXPERTS * bm
    wcur = e % 2
    n_chunks = D_MODEL // W_CHUNK

    def chunk_copies(ex, c):
        rows = _chunk_rows(c)
        s = c % 2
        return (pltpu.make_async_copy(w1_hbm.at[ex, rows, :], st1.at[s], wsem1.at[s]),
                pltpu.make_async_copy(w2_hbm.at[ex, rows, :], st2.at[s], wsem2.at[s]))

    def chunk_start(ex, c):
        for cp in chunk_copies(ex, c):
            cp.start()

    def chunk_finish(ex, c, wslot):
        for cp in chunk_copies(ex, c):
            cp.wait()
        rows = _chunk_rows(c)
        wb1[wslot, rows, :] = st1[c % 2].astype(BF16)
        wb2[wslot, rows, :] = st2[c % 2].astype(BF16)

    def stream_next(c):
        c = jnp.asarray(c, jnp.int32)

        @pl.when(jnp.logical_and(c < n_chunks, e + 1 < n_exp))
        def _():
            chunk_finish(e + 1, c, 1 - wcur)

            @pl.when(c + 2 < n_chunks)
            def _():
                chunk_start(e + 1, c + 2)

    def gather(blk, slot):
        for r in range(bm):
            src = pl.multiple_of(tok_ref[blk * bm + r], PACK_ROWS)
            tile[slot, pl.ds(PACK_ROWS * r, PACK_ROWS), :] = x_ref[pl.ds(src, PACK_ROWS), :]

    def scatter_start(blk, slot):
        for r in range(bm):
            dst = row_ref[blk * bm + r]
            pltpu.make_async_copy(ybuf.at[slot, pl.ds(r, 1), :], ys_hbm.at[pl.ds(dst, 1), :],
                                  osem.at[slot]).start()

    def block_copy(slot, row0):
        return pltpu.make_async_copy(ybuf.at[slot], ys_hbm.at[pl.ds(row0, bm), :], osem.at[slot])

    def scatter_wait(slot):
        block_copy(slot, 0).wait()

    def mlp(slot):
        xb = _unpack_rows(tile, slot, bm)
        hcat = jnp.dot(xb, wb1[wcur], preferred_element_type=F32) + b1_ref[0]
        h_glu = jnp.minimum(hcat[:, :D_FF], SWIGLU_LIMIT)
        h_lin = jnp.clip(hcat[:, D_FF:], -SWIGLU_LIMIT, SWIGLU_LIMIT)
        act = h_glu * jax.nn.sigmoid(SWIGLU_ALPHA * h_glu) * (h_lin + 1.0)
        return jnp.dot(act.astype(BF16), wb2[wcur], preferred_element_type=F32) + b2_ref[0]

    @pl.when(e == 0)
    def _():
        chunk_start(0, 0)
        chunk_start(0, 1)
        for c in range(n_chunks):
            chunk_finish(0, c, 0)
            if c + 2 < n_chunks:
                chunk_start(0, c + 2)
        ybuf[...] = jnp.zeros(ybuf.shape, F32)
        for i in range(N_EXPERTS):
            block_copy(0, n_planes_rows + i * bm).start()
        for i in range(N_EXPERTS):
            block_copy(0, n_planes_rows + i * bm).wait()

    @pl.when(e + 1 < n_exp)
    def _():
        chunk_start(e + 1, 0)
        chunk_start(e + 1, 1)

    @pl.when(nb > 0)
    def _():
        @pl.when(b0 == 0)
        def _():
            gather(0, 0)
            y = mlp(0)
            gather(1, 1)
            ybuf[0] = y
            block_copy(1, spare_block).start()
            stream_next(0)

        def block(i, carry):
            blk = b0 + i
            slot = blk % 2
            y = mlp(slot)
            gather(blk + 1, 1 - slot)
            scatter_start(blk - 1, 1 - slot)
            scatter_wait(slot)
            ybuf[slot] = y
            stream_next(i)
            return carry

        lax.fori_loop(jnp.where(b0 == 0, 1, 0), nb, block, 0)

    lax.fori_loop(jnp.minimum(nb, n_chunks), n_chunks, lambda c, carry: (stream_next(c), carry)[1], 0)

    @pl.when(e == pl.num_programs(0) - 1)
    def _():
        last = (total - 1) % 2
        scatter_start(total - 1, last)
        scatter_wait(1 - last)
        scatter_wait(last)


def _experts(u2p, blk_cnt, blk_start, blk_total, slot_tok, slot_row, n_tok, w1, b1, w2, b2):
    d = D_MODEL
    emap = lambda e, *_: (e, 0, 0)
    n_rows = TOP_K * n_tok + (N_EXPERTS + 1) * MOE_BM
    grid_spec = pltpu.PrefetchScalarGridSpec(
        num_scalar_prefetch=5,
        grid=(N_EXPERTS,),
        in_specs=[pl.BlockSpec(u2p.shape, lambda e, *_: (0, 0), pipeline_mode=pl.Buffered(1)),
                  pl.BlockSpec(memory_space=pl.ANY),
                  pl.BlockSpec((1, 1, 2 * D_FF), emap),
                  pl.BlockSpec(memory_space=pl.ANY),
                  pl.BlockSpec((1, 1, d), emap)],
        out_specs=pl.BlockSpec(memory_space=pl.ANY),
        scratch_shapes=[pltpu.VMEM((2, MOE_BM * PACK_ROWS, LANES), U32),
                        pltpu.VMEM((2, MOE_BM, d), F32),
                        pltpu.VMEM((2, d, 2 * D_FF), BF16),
                        pltpu.VMEM((2, D_FF, d), BF16),
                        pltpu.VMEM((2, W_CHUNK, 2 * D_FF), F32),
                        pltpu.VMEM((2, W_CHUNK, d), F32),
                        pltpu.SemaphoreType.DMA((2,)),
                        pltpu.SemaphoreType.DMA((2,)),
                        pltpu.SemaphoreType.DMA((2,))],
    )
    return pl.pallas_call(
        _expert_kernel,
        grid_spec=grid_spec,
        out_shape=jax.ShapeDtypeStruct((n_rows, d), F32),
        compiler_params=pltpu.CompilerParams(vmem_limit_bytes=VMEM_LIMIT,
                                             dimension_semantics=("arbitrary",)),
        name="experts",
    )(blk_cnt, blk_start, blk_total, slot_tok, slot_row, u2p,
      w1.astype(F32), b1.reshape(N_EXPERTS, 1, 2 * D_FF), w2.astype(F32),
      b2.reshape(N_EXPERTS, 1, d))


def _combine_kernel(h1_ref, gate_ref, g_ref, y0_ref, y1_ref, y2_ref, y3_ref, out_ref):
    acc = h1_ref[...]
    gates = gate_ref[...]
    for kk, y_ref in enumerate((y0_ref, y1_ref, y2_ref, y3_ref)):
        acc = acc + gates[:, kk:kk + 1] * y_ref[...]
    out_ref[...] = _rms_norm(acc, g_ref[...])


def _combine(h1, gates, final_g, ys):
    n, d = h1.shape
    tt = COMB_TILE
    nt = n // tt
    plane = lambda kk: pl.BlockSpec((tt, d), lambda i, kk=kk: (kk * nt + i, 0))
    return pl.pallas_call(
        _combine_kernel,
        grid=(nt,),
        in_specs=[pl.BlockSpec((tt, d), lambda i: (i, 0)),
                  pl.BlockSpec((tt, LANES), lambda i: (i, 0)),
                  pl.BlockSpec((1, d), lambda i: (0, 0))] + [plane(kk) for kk in range(TOP_K)],
        out_specs=pl.BlockSpec((tt, d), lambda i: (i, 0)),
        out_shape=jax.ShapeDtypeStruct((n, d), F32),
        compiler_params=pltpu.CompilerParams(vmem_limit_bytes=VMEM_LIMIT,
                                             dimension_semantics=("arbitrary",)),
        name="combine",
    )(h1, gates, final_g.reshape(1, d).astype(F32), ys, ys, ys, ys)


def _routing_tables(top_idx):
    n_tok = top_idx.shape[0]
    n_asg = n_tok * TOP_K
    bm = MOE_BM
    flat_e = top_idx.reshape(-1).astype(jnp.int32)
    sorted_e, order = lax.sort((flat_e, jnp.arange(n_asg, dtype=jnp.int32)), num_keys=1)
    experts = jnp.arange(N_EXPERTS, dtype=jnp.int32)
    counts = jnp.sum((flat_e[:, None] == experts[None, :]).astype(jnp.int32), axis=0)
    grp_start = jnp.cumsum(counts).astype(jnp.int32) - counts
    blk_cnt = (counts + bm - 1) // bm
    blk_end = jnp.cumsum(blk_cnt).astype(jnp.int32)
    blk_start = blk_end - blk_cnt
    n_blocks = n_asg // bm + N_EXPERTS
    blk = jnp.arange(n_blocks, dtype=jnp.int32)
    blk_e = jnp.minimum(jnp.sum((blk[:, None] >= blk_end[None, :]).astype(jnp.int32), axis=1),
                        N_EXPERTS - 1)
    e_count = counts[blk_e][:, None]
    e_first = grp_start[blk_e][:, None]
    within = ((blk - blk_start[blk_e]) * bm)[:, None] + jnp.arange(bm, dtype=jnp.int32)[None, :]
    real = within < e_count
    asg = order[jnp.where(real, e_first + within, 0)]
    tok = asg // TOP_K
    pad_ord = blk[:, None] * bm + jnp.arange(bm, dtype=jnp.int32)[None, :] - (
        e_first + jnp.minimum(within, e_count))
    slot_tok = jnp.where(real, tok * PACK_ROWS, 0).astype(jnp.int32).reshape(-1)
    slot_row = jnp.where(real, (asg % TOP_K) * n_tok + tok,
                         n_asg + pad_ord).astype(jnp.int32).reshape(-1)
    return blk_cnt, blk_start, blk_end[-1:], slot_tok, slot_row


def kernel(x, meta_tokens, norm_mix_g, w_in, b_in, conv_dw_w, conv_dw_b, conv_ln_g, conv_ln_b, conv_pw_w, conv_pw_b, mlstm_norm_g, mlstm_out_w, w_out, norm_ffn_g, router_w, router_b, expert_w1, expert_b1, expert_w2, expert_b2, final_norm_g):
    bsz, seq, d = x.shape
    n_tok = bsz * seq
    assert d == D_MODEL and seq % SEQ_TILE == 0 and n_tok % COMB_TILE == 0 and w_in.shape[0] == 1
    weights = _mixer_weights(norm_mix_g, w_in, b_in, conv_dw_w, conv_dw_b, conv_ln_g, conv_ln_b,
                             conv_pw_w, conv_pw_b, mlstm_norm_g, mlstm_out_w, w_out, norm_ffn_g,
                             router_w, router_b)
    h1, u2p, idx, gates = _mixer(x.astype(F32), meta_tokens, weights)
    h1 = h1.reshape(n_tok, d)
    gates = gates.reshape(n_tok, LANES)
    top_idx = idx.reshape(n_tok, LANES)[:, :TOP_K]
    blk_cnt, blk_start, blk_total, slot_tok, slot_row = _routing_tables(top_idx)
    ys = _experts(u2p, blk_cnt, blk_start, blk_total, slot_tok, slot_row, n_tok,
                  expert_w1[0], expert_b1[0], expert_w2[0], expert_b2[0])
    out = _combine(h1, gates, final_norm_g, ys)
    return out.reshape(bsz, seq, d)
```

```python
import jax
import jax.numpy as jnp
from jax import lax
from jax.experimental import pallas as pl
from jax.experimental.pallas import tpu as pltpu

F32 = jnp.float32
BF16 = jnp.bfloat16
U32 = jnp.uint32

D_MODEL = 1024
N_META = 16
CONV_WIDTH = 31
M_HEADS = 4
M_DK = 128
M_DV = 256
QK_SCALE = M_DK ** -0.5
N_EXPERTS = 32
TOP_K = 4
D_FF = D_MODEL
SWIGLU_ALPHA = 1.702
SWIGLU_LIMIT = 7.0
NORM_EPS = 1e-5
NEG_GATE = -1.0e4

LANES = 128
SUBLANES = 8
VMEM_LIMIT = 56 * 1024 * 1024

SEQ_TILE = 256
CONV_HIST = 32
CONV_ROWS = 64
PROJ_CHUNK = 256
MOE_BM = 128
W_CHUNK = 128
COMB_TILE = 512
PACK = 2
PACK_ROWS = D_MODEL // (PACK * LANES)

C_GLU = 0
C_Q = C_GLU + 2 * D_MODEL
C_K = C_Q + M_HEADS * M_DK
C_V = C_K + M_HEADS * M_DK
C_O = C_V + M_HEADS * M_DV
C_GC = C_O + D_MODEL
C_GM = C_GC + D_MODEL
C_IF = C_GM + D_MODEL
N_PROJ = C_IF + LANES


def _rms_norm(x, g):
    return x * lax.rsqrt(jnp.mean(x * x, axis=-1, keepdims=True) + NORM_EPS) * g


def _split_bf16(x, terms):
    parts = []
    for _ in range(terms - 1):
        p = x.astype(BF16)
        parts.append(p)
        x = x - p.astype(F32)
    parts.append(x.astype(BF16))
    return parts


def _proj(u_bf, w, lo, hi):
    return jnp.dot(u_bf, w["win"][:, lo:hi], preferred_element_type=F32) + w["bin"][:, lo:hi]


def _glu(u_bf, w, row_valid):
    zg = _proj(u_bf, w, C_GLU, C_Q)
    y = zg[:, :D_MODEL] * jax.nn.sigmoid(zg[:, D_MODEL:])
    if row_valid is not None:
        y = jnp.where(row_valid, y, 0.0)
    return y


def _conv_block(ybuf, dww_ref, dwb_ref, c, j):
    first = CONV_HIST - (CONV_WIDTH - 1)
    cs = slice(j * LANES, (j + 1) * LANES)
    win = ybuf[c * CONV_ROWS:c * CONV_ROWS + CONV_ROWS + CONV_HIST, cs]
    acc = jnp.broadcast_to(dwb_ref[:, cs], (CONV_ROWS, LANES))
    for phase in range(SUBLANES):
        taps = [k for k in range(CONV_WIDTH) if (-(first + k)) % SUBLANES == phase]
        if not taps:
            continue
        wb = win if phase == 0 else pltpu.roll(win, phase, axis=0)
        for k in taps:
            i0 = first + k + phase
            acc = acc + dww_ref[k:k + 1, cs] * wb[i0:i0 + CONV_ROWS, :]
    return acc


def _mlstm(zcol, w, c_st, n_st, m_st, row_valid, want_h):
    t = SEQ_TILE
    zif = zcol(C_IF, N_PROJ)
    ig_all = zif
    lf_all = jax.nn.log_sigmoid(zif)
    if row_valid is not None:
        ig_all = jnp.where(row_valid, ig_all, NEG_GATE)
        lf_all = jnp.where(row_valid, lf_all, 0.0)
    rows = lax.broadcasted_iota(jnp.int32, (t, t), 0)
    cols = lax.broadcasted_iota(jnp.int32, (t, t), 1)
    causal = cols <= rows
    lf_terms = jnp.concatenate(_split_bf16(lf_all, 3), axis=1)
    b3 = jnp.dot(causal.astype(BF16), lf_terms, preferred_element_type=F32)
    b_all = b3[:, 0:LANES] + b3[:, LANES:2 * LANES] + b3[:, 2 * LANES:3 * LANES]
    b_sh = pltpu.roll(b_all, LANES - M_HEADS, axis=1)
    a_all = ig_all - b_sh
    a_rows = a_all.T
    hs = []
    for h in range(M_HEADS):
        a_row = a_rows[h:h + 1, :]
        a_col = a_all[:, h:h + 1]
        b_col = b_sh[:, h:h + 1]
        m_prev = m_st[h][0:1, 0:1]
        n_prev = n_st[h][0:1, :]
        ct = c_st[h]
        k = zcol(C_K + h * M_DK, C_K + (h + 1) * M_DK)
        v_bf = zcol(C_V + h * M_DV, C_V + (h + 1) * M_DV).astype(BF16)
        amat = jnp.where(causal, a_row, -jnp.inf)
        m_run = jnp.maximum(jnp.max(amat, axis=-1, keepdims=True), m_prev)
        if want_h:
            q = zcol(C_Q + h * M_DK, C_Q + (h + 1) * M_DK) * QK_SCALE
            q_bf = q.astype(BF16)
            s = lax.dot_general(q_bf, k.astype(BF16), (((1,), (1,)), ((), ())),
                                preferred_element_type=F32)
            p = s * jnp.exp(amat - m_run)
            inter_sc = jnp.exp(m_prev - m_run)
            num = (jnp.dot(p.astype(BF16), v_bf, preferred_element_type=F32)
                   + inter_sc * jnp.dot(q_bf, ct.astype(BF16), preferred_element_type=F32))
            den = (jnp.sum(p, axis=-1, keepdims=True)
                   + inter_sc * jnp.sum(q * n_prev, axis=-1, keepdims=True))
            scale = 1.0 / jnp.maximum(jnp.abs(den), jnp.exp(-(b_col + m_run)))
            hh = num * scale
            hh = hh * lax.rsqrt(jnp.mean(hh * hh, axis=-1, keepdims=True) + NORM_EPS)
            hs.append(hh * w["mng"][:, h * M_DV:(h + 1) * M_DV])
        m_last = m_run[t - 1:t, :]
        decay = jnp.exp(m_prev - m_last)
        kw = k * jnp.exp(a_col - m_last)
        c_st[h] = decay * ct + jnp.dot(kw.T.astype(BF16), v_bf, preferred_element_type=F32)
        n_new = decay * n_prev + jnp.sum(kw, axis=0, keepdims=True)
        n_st[h] = jnp.broadcast_to(n_new, (SUBLANES, M_DK))
        m_st[h] = jnp.broadcast_to(b_col[t - 1:t, :] + m_last, (SUBLANES, LANES))
    return jnp.concatenate(hs, axis=1) if want_h else None


def _router(u2, rw_ref, rb_ref):
    t = u2.shape[0]
    u_hi, u_mid = _split_bf16(u2, 2)
    lhs = jnp.concatenate([u_hi, u_mid, u_hi], axis=1)
    logits_t = lax.dot_general(rw_ref[...], lhs, (((1,), (1,)), ((), ())),
                               preferred_element_type=F32)
    cur = logits_t[0:N_EXPERTS, :] + rb_ref[:, 0:1]
    row = lax.broadcasted_iota(jnp.int32, (N_EXPERTS, t), 0)
    idx_rows = []
    exp_rows = []
    top0 = None
    esum = None
    for kk in range(TOP_K):
        mx = jnp.max(cur, axis=0, keepdims=True)
        ix = jnp.min(jnp.where(cur == mx, row, N_EXPERTS), axis=0, keepdims=True)
        if kk == 0:
            top0 = mx
        ek = jnp.exp(mx - top0)
        esum = ek if kk == 0 else esum + ek
        idx_rows.append(ix)
        exp_rows.append(ek)
        cur = jnp.where(row == ix, -jnp.inf, cur)
    idx_t = jnp.concatenate(idx_rows + [jnp.zeros((SUBLANES - TOP_K, t), jnp.int32)], axis=0)
    gates_t = jnp.concatenate([r / esum for r in exp_rows]
                              + [jnp.zeros((LANES - TOP_K, t), F32)], axis=0)
    return idx_t, gates_t.T


def _pack_rows(u2, out_ref):
    t = u2.shape[0]
    bits = lax.bitcast_convert_type(u2.astype(BF16).astype(F32), U32)
    for i in range(PACK_ROWS):
        lo = bits[:, i * 2 * LANES:i * 2 * LANES + LANES]
        hi = bits[:, i * 2 * LANES + LANES:(i + 1) * 2 * LANES]
        out_ref[pl.ds(i, t, stride=PACK_ROWS), :] = hi | (lo >> 16)


def _unpack_rows(tile_ref, slot, rows):
    parts = []
    for i in range(PACK_ROWS):
        word = tile_ref.at[slot][pl.ds(i, rows, stride=PACK_ROWS), :]
        lo = lax.bitcast_convert_type(word << 16, F32)
        hi = lax.bitcast_convert_type(word & jnp.uint32(0xFFFF0000), F32)
        parts += [lo.astype(BF16), hi.astype(BF16)]
    return jnp.concatenate(parts, axis=1)


def _weights(refs):
    names = ("g1", "win", "bin", "dww", "dwb", "lng", "lnb", "pww", "pwb", "mng", "mow",
             "wout", "g2", "rw", "rb")
    return dict(zip(names, refs))


N_WEIGHTS = 15


def _prefix_kernel(*refs):
    x_ref = refs[0]
    w = _weights(refs[1:1 + N_WEIGHTS])
    ytail_ref, c_out, n_out, m_out = refs[1 + N_WEIGHTS:]
    t = SEQ_TILE
    row_valid = lax.broadcasted_iota(jnp.int32, (t, 1), 0) >= t - N_META
    c_out[...] = jnp.zeros(c_out.shape, F32)
    n_out[...] = jnp.zeros(n_out.shape, F32)
    m_out[...] = jnp.zeros(m_out.shape, F32)
    u_bf = _rms_norm(x_ref[...], w["g1"][...]).astype(BF16)
    y = _glu(u_bf, w, row_valid)
    ytail_ref[...] = y[t - CONV_HIST:, :]
    _mlstm(lambda lo, hi: _proj(u_bf, w, lo, hi), w, c_out, n_out, m_out, row_valid, want_h=False)


def _mixer_kernel(*refs):
    x_ref = refs[0]
    w = _weights(refs[1:1 + N_WEIGHTS])
    y0_ref, c0_ref, n0_ref, m0_ref = refs[1 + N_WEIGHTS:5 + N_WEIGHTS]
    h1_ref, u2p_ref, idx_ref, gate_ref = refs[5 + N_WEIGHTS:9 + N_WEIGHTS]
    ybuf, cbuf, zbuf, c_st, n_st, m_st = refs[9 + N_WEIGHTS:]
    t = SEQ_TILE

    @pl.when(pl.program_id(1) == 0)
    def _():
        ybuf[0:CONV_HIST, :] = y0_ref[...]
        c_st[...] = c0_ref[...]
        n_st[...] = n0_ref[...]
        m_st[...] = m0_ref[...]

    x = x_ref[0]
    u_bf = _rms_norm(x, w["g1"][...]).astype(BF16)

    ybuf[CONV_HIST:CONV_HIST + t, :] = _glu(u_bf, w, None)
    chunks = [(lo, min(lo + PROJ_CHUNK, N_PROJ)) for lo in range(C_Q, N_PROJ, PROJ_CHUNK)]
    blocks = [(c, j) for c in range(t // CONV_ROWS) for j in range(D_MODEL // LANES)]
    assert len(chunks) <= len(blocks)
    for n, (c, j) in enumerate(blocks):
        cbuf[c * CONV_ROWS:(c + 1) * CONV_ROWS, j * LANES:(j + 1) * LANES] = _conv_block(
            ybuf, w["dww"], w["dwb"], c, j)
        if n < len(chunks):
            lo, hi = chunks[n]
            zbuf[:, lo - C_Q:hi - C_Q] = _proj(u_bf, w, lo, hi)
    ybuf[0:CONV_HIST, :] = ybuf[t:t + CONV_HIST, :]
    zcol = lambda lo, hi: zbuf[:, lo - C_Q:hi - C_Q]

    conv = cbuf[...]
    mu = jnp.mean(conv, axis=-1, keepdims=True)
    cen = conv - mu
    var = jnp.mean(cen * cen, axis=-1, keepdims=True)
    ln = cen * lax.rsqrt(var + NORM_EPS) * w["lng"][...] + w["lnb"][...]
    conv_out = jnp.dot(jax.nn.silu(ln).astype(BF16), w["pww"][...],
                       preferred_element_type=F32) + w["pwb"][...]

    hcat = _mlstm(zcol, w, c_st, n_st, m_st, None, want_h=True)
    o_gate = jax.nn.sigmoid(zcol(C_O, C_GC))
    mlstm_out = jnp.dot((o_gate * hcat).astype(BF16), w["mow"][...], preferred_element_type=F32)

    g_conv = jax.nn.sigmoid(zcol(C_GC, C_GM))
    g_mlstm = jax.nn.sigmoid(zcol(C_GM, C_IF))
    mix = (g_conv * conv_out + g_mlstm * mlstm_out).astype(BF16)
    h1 = x + jnp.dot(mix, w["wout"][...], preferred_element_type=F32)
    h1_ref[0] = h1

    u2 = _rms_norm(h1, w["g2"][...])
    _pack_rows(u2, u2p_ref)
    idx, gates = _router(u2, w["rw"], w["rb"])
    idx_ref[...] = idx
    gate_ref[0] = gates


def _const_spec(shape):
    nd = len(shape)
    return pl.BlockSpec(shape, lambda *_: (0,) * nd, pipeline_mode=pl.Buffered(1))


def _mixer_weights(norm_mix_g, w_in, b_in, conv_dw_w, conv_dw_b, conv_ln_g, conv_ln_b, conv_pw_w,
                   conv_pw_b, mlstm_norm_g, mlstm_out_w, w_out, norm_ffn_g, router_w, router_b):
    d = D_MODEL
    qo, ko, vo = 2 * d, 2 * d + 512, 2 * d + 1024
    io = vo + d
    oo = io + 2 * M_HEADS

    def regroup(a):
        parts = [a[..., 0:qo], a[..., qo:ko], a[..., ko:vo], a[..., vo:io],
                 a[..., oo:oo + d], a[..., oo + d:oo + 2 * d], a[..., oo + 2 * d:oo + 3 * d],
                 a[..., io:oo],
                 jnp.zeros(a.shape[:-1] + (LANES - 2 * M_HEADS,), a.dtype)]
        return jnp.concatenate(parts, axis=-1)

    row = lambda a: a.reshape(1, -1).astype(F32)
    dww = jnp.pad(conv_dw_w[0].astype(F32), ((0, 32 - CONV_WIDTH), (0, 0)))
    rw = jnp.pad(router_w[0].astype(F32), ((0, 0), (0, LANES - N_EXPERTS)))
    rw_hi = rw.astype(BF16)
    rw_mid = (rw - rw_hi.astype(F32)).astype(BF16)
    rb = jnp.broadcast_to(router_b[0].astype(F32)[:, None], (N_EXPERTS, LANES))
    return [row(norm_mix_g[0]), regroup(w_in[0].astype(BF16)), row(regroup(b_in[0])),
            dww, row(conv_dw_b[0]), row(conv_ln_g[0]), row(conv_ln_b[0]),
            conv_pw_w[0].astype(BF16), row(conv_pw_b[0]), row(mlstm_norm_g[0]),
            mlstm_out_w[0].astype(BF16), w_out[0].astype(BF16), row(norm_ffn_g[0]),
            jnp.concatenate([rw_hi, rw_hi, rw_mid], axis=0).T, rb]


def _mixer(x, meta_tokens, weights):
    bsz, seq, d = x.shape
    t = SEQ_TILE
    n_seq = seq // t
    w_specs = [_const_spec(a.shape) for a in weights]
    state_shapes = [(CONV_HIST, d), (M_HEADS, M_DK, M_DV), (M_HEADS, SUBLANES, M_DK),
                    (M_HEADS, SUBLANES, LANES)]

    x_meta = jnp.concatenate([jnp.zeros((t - N_META, d), F32), meta_tokens.astype(F32)], axis=0)
    state = pl.pallas_call(
        _prefix_kernel,
        grid=(1,),
        in_specs=[pl.BlockSpec((t, d), lambda i: (0, 0))] + w_specs,
        out_specs=[pl.BlockSpec(s, lambda i, n=len(s): (0,) * n) for s in state_shapes],
        out_shape=[jax.ShapeDtypeStruct(s, F32) for s in state_shapes],
        compiler_params=pltpu.CompilerParams(vmem_limit_bytes=VMEM_LIMIT),
        name="prefix",
    )(x_meta, *weights)

    tile = lambda b, s: (b, s, 0)
    return pl.pallas_call(
        _mixer_kernel,
        grid=(bsz, n_seq),
        in_specs=([pl.BlockSpec((1, t, d), tile)] + w_specs
                  + [_const_spec(s) for s in state_shapes]),
        out_specs=[pl.BlockSpec((1, t, d), tile),
                   pl.BlockSpec((t * PACK_ROWS, LANES), lambda b, s: (b * n_seq + s, 0)),
                   pl.BlockSpec((SUBLANES, t), lambda b, s: (0, b * n_seq + s)),
                   pl.BlockSpec((1, t, LANES), tile)],
        out_shape=[jax.ShapeDtypeStruct((bsz, seq, d), F32),
                   jax.ShapeDtypeStruct((bsz * seq * PACK_ROWS, LANES), U32),
                   jax.ShapeDtypeStruct((SUBLANES, bsz * seq), jnp.int32),
                   jax.ShapeDtypeStruct((bsz, seq, LANES), F32)],
        scratch_shapes=[pltpu.VMEM((CONV_HIST + t, d), F32),
                        pltpu.VMEM((t, d), F32),
                        pltpu.VMEM((t, N_PROJ - C_Q), F32),
                        pltpu.VMEM((M_HEADS, M_DK, M_DV), F32),
                        pltpu.VMEM((M_HEADS, SUBLANES, M_DK), F32),
                        pltpu.VMEM((M_HEADS, SUBLANES, LANES), F32)],
        compiler_params=pltpu.CompilerParams(vmem_limit_bytes=VMEM_LIMIT,
                                             dimension_semantics=("arbitrary", "arbitrary")),
        name="mixer",
    )(x, *weights, *state)


def _chunk_rows(c):
    start = c * W_CHUNK
    return pl.ds(start if isinstance(c, int) else pl.multiple_of(start, W_CHUNK), W_CHUNK)


def _expert_kernel(cnt_ref, start_ref, total_ref, tok_ref, row_ref,
                   x_ref, w1_hbm, b1_ref, w2_hbm, b2_ref, ys_hbm,
                   tile, ybuf, wb1, wb2, st1, st2, osem, wsem1, wsem2):
    e = pl.program_id(0)
    n_exp = pl.num_programs(0)
    nb = cnt_ref[e]
    b0 = start_ref[e]
    total = total_ref[0]
    bm = MOE_BM
    n_planes_rows = ys_hbm.shape[0] - (N_EXPERTS + 1) * bm
    spare_block = n_planes_rows + N_EXPERTS * bm
    wcur = e % 2
    n_chunks = D_MODEL // W_CHUNK

    def chunk_copies(ex, c):
        rows = _chunk_rows(c)
        s = c % 2
        return (pltpu.make_async_copy(w1_hbm.at[ex, rows, :], st1.at[s], wsem1.at[s]),
                pltpu.make_async_copy(w2_hbm.at[ex, rows, :], st2.at[s], wsem2.at[s]))

    def chunk_start(ex, c):
        for cp in chunk_copies(ex, c):
            cp.start()

    def chunk_finish(ex, c, wslot):
        for cp in chunk_copies(ex, c):
            cp.wait()
        rows = _chunk_rows(c)
        wb1[wslot, rows, :] = st1[c % 2].astype(BF16)
        wb2[wslot, rows, :] = st2[c % 2].astype(BF16)

    def stream_next(c):
        c = jnp.asarray(c, jnp.int32)

        @pl.when(jnp.logical_and(c < n_chunks, e + 1 < n_exp))
        def _():
            chunk_finish(e + 1, c, 1 - wcur)

            @pl.when(c + 2 < n_chunks)
            def _():
                chunk_start(e + 1, c + 2)

    def gather(blk, slot):
        for r in range(bm):
            src = pl.multiple_of(tok_ref[blk * bm + r], PACK_ROWS)
            tile[slot, pl.ds(PACK_ROWS * r, PACK_ROWS), :] = x_ref[pl.ds(src, PACK_ROWS), :]

    def scatter_start(blk, slot):
        for r in range(bm):
            dst = row_ref[blk * bm + r]
            pltpu.make_async_copy(ybuf.at[slot, pl.ds(r, 1), :], ys_hbm.at[pl.ds(dst, 1), :],
                                  osem.at[slot]).start()

    def block_copy(slot, row0):
        return pltpu.make_async_copy(ybuf.at[slot], ys_hbm.at[pl.ds(row0, bm), :], osem.at[slot])

    def scatter_wait(slot):
        block_copy(slot, 0).wait()

    def mlp(slot):
        xb = _unpack_rows(tile, slot, bm)
        hcat = jnp.dot(xb, wb1[wcur], preferred_element_type=F32) + b1_ref[0]
        h_glu = jnp.minimum(hcat[:, :D_FF], SWIGLU_LIMIT)
        h_lin = jnp.clip(hcat[:, D_FF:], -SWIGLU_LIMIT, SWIGLU_LIMIT)
        act = h_glu * jax.nn.sigmoid(SWIGLU_ALPHA * h_glu) * (h_lin + 1.0)
        return jnp.dot(act.astype(BF16), wb2[wcur], preferred_element_type=F32) + b2_ref[0]

    @pl.when(e == 0)
    def _():
        chunk_start(0, 0)
        chunk_start(0, 1)
        for c in range(n_chunks):
            chunk_finish(0, c, 0)
            if c + 2 < n_chunks:
                chunk_start(0, c + 2)
        ybuf[...] = jnp.zeros(ybuf.shape, F32)
        for i in range(N_EXPERTS):
            block_copy(0, n_planes_rows + i * bm).start()
        for i in range(N_EXPERTS):
            block_copy(0, n_planes_rows + i * bm).wait()

    @pl.when(e + 1 < n_exp)
    def _():
        chunk_start(e + 1, 0)
        chunk_start(e + 1, 1)

    @pl.when(nb > 0)
    def _():
        @pl.when(b0 == 0)
        def _():
            gather(0, 0)
            y = mlp(0)
            gather(1, 1)
            ybuf[0] = y
            block_copy(1, spare_block).start()
            stream_next(0)

        def block_body(blk, slot):
            y = mlp(slot)
            gather(blk + 1, 1 - slot)
            scatter_start(blk - 1, 1 - slot)
            scatter_wait(slot)
            ybuf[slot] = y

        def block(i, carry):
            blk = b0 + i
            for slot in range(2):
                @pl.when(blk % 2 == slot)
                def _():
                    block_body(blk, slot)
            stream_next(i)
            return carry

        lax.fori_loop(jnp.where(b0 == 0, 1, 0), nb, block, 0)

    lax.fori_loop(jnp.minimum(nb, n_chunks), n_chunks, lambda c, carry: (stream_next(c), carry)[1], 0)

    @pl.when(e == pl.num_programs(0) - 1)
    def _():
        for last in range(2):
            @pl.when((total - 1) % 2 == last)
            def _():
                scatter_start(total - 1, last)
        scatter_wait(0)
        scatter_wait(1)


def _experts(u2p, blk_cnt, blk_start, blk_total, slot_tok, slot_row, n_tok, w1, b1, w2, b2):
    d = D_MODEL
    emap = lambda e, *_: (e, 0, 0)
    n_rows = TOP_K * n_tok + (N_EXPERTS + 1) * MOE_BM
    grid_spec = pltpu.PrefetchScalarGridSpec(
        num_scalar_prefetch=5,
        grid=(N_EXPERTS,),
        in_specs=[pl.BlockSpec(u2p.shape, lambda e, *_: (0, 0), pipeline_mode=pl.Buffered(1)),
                  pl.BlockSpec(memory_space=pl.ANY),
                  pl.BlockSpec((1, 1, 2 * D_FF), emap),
                  pl.BlockSpec(memory_space=pl.ANY),
                  pl.BlockSpec((1, 1, d), emap)],
        out_specs=pl.BlockSpec(memory_space=pl.ANY),
        scratch_shapes=[pltpu.VMEM((2, MOE_BM * PACK_ROWS, LANES), U32),
                        pltpu.VMEM((2, MOE_BM, d), F32),
                        pltpu.VMEM((2, d, 2 * D_FF), BF16),
                        pltpu.VMEM((2, D_FF, d), BF16),
                        pltpu.VMEM((2, W_CHUNK, 2 * D_FF), F32),
                        pltpu.VMEM((2, W_CHUNK, d), F32),
                        pltpu.SemaphoreType.DMA((2,)),
                        pltpu.SemaphoreType.DMA((2,)),
                        pltpu.SemaphoreType.DMA((2,))],
    )
    return pl.pallas_call(
        _expert_kernel,
        grid_spec=grid_spec,
        out_shape=jax.ShapeDtypeStruct((n_rows, d), F32),
        compiler_params=pltpu.CompilerParams(vmem_limit_bytes=VMEM_LIMIT,
                                             dimension_semantics=("arbitrary",)),
        name="experts",
    )(blk_cnt, blk_start, blk_total, slot_tok, slot_row, u2p,
      w1.astype(F32), b1.reshape(N_EXPERTS, 1, 2 * D_FF), w2.astype(F32),
      b2.reshape(N_EXPERTS, 1, d))


def _combine_kernel(h1_ref, gate_ref, g_ref, y0_ref, y1_ref, y2_ref, y3_ref, out_ref):
    acc = h1_ref[...]
    gates = gate_ref[...]
    for kk, y_ref in enumerate((y0_ref, y1_ref, y2_ref, y3_ref)):
        acc = acc + gates[:, kk:kk + 1] * y_ref[...]
    out_ref[...] = _rms_norm(acc, g_ref[...])


def _combine(h1, gates, final_g, ys):
    n, d = h1.shape
    tt = COMB_TILE
    nt = n // tt
    plane = lambda kk: pl.BlockSpec((tt, d), lambda i, kk=kk: (kk * nt + i, 0))
    return pl.pallas_call(
        _combine_kernel,
        grid=(nt,),
        in_specs=[pl.BlockSpec((tt, d), lambda i: (i, 0)),
                  pl.BlockSpec((tt, LANES), lambda i: (i, 0)),
                  pl.BlockSpec((1, d), lambda i: (0, 0))] + [plane(kk) for kk in range(TOP_K)],
        out_specs=pl.BlockSpec((tt, d), lambda i: (i, 0)),
        out_shape=jax.ShapeDtypeStruct((n, d), F32),
        compiler_params=pltpu.CompilerParams(vmem_limit_bytes=VMEM_LIMIT,
                                             dimension_semantics=("arbitrary",)),
        name="combine",
    )(h1, gates, final_g.reshape(1, d).astype(F32), ys, ys, ys, ys)


def _routing_tables(top_idx):
    n_tok = top_idx.shape[1]
    n_asg = n_tok * TOP_K
    bm = MOE_BM
    flat_e = top_idx.reshape(-1).astype(jnp.int32)
    sorted_e, order = lax.sort((flat_e, jnp.arange(n_asg, dtype=jnp.int32)), num_keys=1)
    experts = jnp.arange(N_EXPERTS, dtype=jnp.int32)
    counts = jnp.sum((flat_e[:, None] == experts[None, :]).astype(jnp.int32), axis=0)
    grp_start = jnp.cumsum(counts).astype(jnp.int32) - counts
    blk_cnt = (counts + bm - 1) // bm
    blk_end = jnp.cumsum(blk_cnt).astype(jnp.int32)
    blk_start = blk_end - blk_cnt
    n_blocks = n_asg // bm + N_EXPERTS
    blk = jnp.arange(n_blocks, dtype=jnp.int32)
    blk_e = jnp.minimum(jnp.sum((blk[:, None] >= blk_end[None, :]).astype(jnp.int32), axis=1),
                        N_EXPERTS - 1)
    e_count = counts[blk_e][:, None]
    e_first = grp_start[blk_e][:, None]
    within = ((blk - blk_start[blk_e]) * bm)[:, None] + jnp.arange(bm, dtype=jnp.int32)[None, :]
    real = within < e_count
    asg = order[jnp.where(real, e_first + within, 0)]
    tok = asg % n_tok
    pad_ord = blk[:, None] * bm + jnp.arange(bm, dtype=jnp.int32)[None, :] - (
        e_first + jnp.minimum(within, e_count))
    slot_tok = jnp.where(real, tok * PACK_ROWS, 0).astype(jnp.int32).reshape(-1)
    slot_row = jnp.where(real, asg,
                         n_asg + pad_ord).astype(jnp.int32).reshape(-1)
    return blk_cnt, blk_start, blk_end[-1:], slot_tok, slot_row


def kernel(x, meta_tokens, norm_mix_g, w_in, b_in, conv_dw_w, conv_dw_b, conv_ln_g, conv_ln_b, conv_pw_w, conv_pw_b, mlstm_norm_g, mlstm_out_w, w_out, norm_ffn_g, router_w, router_b, expert_w1, expert_b1, expert_w2, expert_b2, final_norm_g):
    bsz, seq, d = x.shape
    n_tok = bsz * seq
    assert d == D_MODEL and seq % SEQ_TILE == 0 and n_tok % COMB_TILE == 0 and w_in.shape[0] == 1
    weights = _mixer_weights(norm_mix_g, w_in, b_in, conv_dw_w, conv_dw_b, conv_ln_g, conv_ln_b,
                             conv_pw_w, conv_pw_b, mlstm_norm_g, mlstm_out_w, w_out, norm_ffn_g,
                             router_w, router_b)
    h1, u2p, idx, gates = _mixer(x.astype(F32), meta_tokens, weights)
    h1 = h1.reshape(n_tok, d)
    gates = gates.reshape(n_tok, LANES)
    top_idx = idx[:TOP_K]
    blk_cnt, blk_start, blk_total, slot_tok, slot_row = _routing_tables(top_idx)
    ys = _experts(u2p, blk_cnt, blk_start, blk_total, slot_tok, slot_row, n_tok,
                  expert_w1[0], expert_b1[0], expert_w2[0], expert_b2[0])
    out = _combine(h1, gates, final_norm_g, ys)
    return out.reshape(bsz, seq, d)
```

```python
import jax
import jax.numpy as jnp
from jax import lax
from jax.experimental import pallas as pl
from jax.experimental.pallas import tpu as pltpu

F32 = jnp.float32
BF16 = jnp.bfloat16
U32 = jnp.uint32

D_MODEL = 1024
N_META = 16
CONV_WIDTH = 31
M_HEADS = 4
M_DK = 128
M_DV = 256
QK_SCALE = M_DK ** -0.5
N_EXPERTS = 32
TOP_K = 4
D_FF = D_MODEL
SWIGLU_ALPHA = 1.702
SWIGLU_LIMIT = 7.0
NORM_EPS = 1e-5
NEG_GATE = -1.0e4

LANES = 128
SUBLANES = 8
VMEM_LIMIT = 56 * 1024 * 1024

SEQ_TILE = 256
CONV_HIST = 32
CONV_ROWS = 64
PROJ_CHUNK = 256
MOE_BM = 128
W_CHUNK = 128
COMB_TILE = 1024
PACK = 2
PACK_ROWS = D_MODEL // (PACK * LANES)

C_GLU = 0
C_Q = C_GLU + 2 * D_MODEL
C_K = C_Q + M_HEADS * M_DK
C_V = C_K + M_HEADS * M_DK
C_O = C_V + M_HEADS * M_DV
C_GC = C_O + D_MODEL
C_GM = C_GC + D_MODEL
C_IF = C_GM + D_MODEL
N_PROJ = C_IF + LANES


def _rms_norm(x, g):
    return x * lax.rsqrt(jnp.mean(x * x, axis=-1, keepdims=True) + NORM_EPS) * g


def _sigmoid(x):
    return 0.5 * jnp.tanh(0.5 * x) + 0.5


def _split_bf16(x, terms):
    parts = []
    for _ in range(terms - 1):
        p = x.astype(BF16)
        parts.append(p)
        x = x - p.astype(F32)
    parts.append(x.astype(BF16))
    return parts


def _proj(u_bf, w, lo, hi):
    return jnp.dot(u_bf, w["win"][:, lo:hi], preferred_element_type=F32) + w["bin"][:, lo:hi]


def _glu(u_bf, w, row_valid):
    zg = _proj(u_bf, w, C_GLU, C_Q)
    y = zg[:, :D_MODEL] * _sigmoid(zg[:, D_MODEL:])
    if row_valid is not None:
        y = jnp.where(row_valid, y, 0.0)
    return y


def _conv_block(ybuf, dww_ref, dwb_ref, c, j):
    first = CONV_HIST - (CONV_WIDTH - 1)
    cs = slice(j * LANES, (j + 1) * LANES)
    win = ybuf[c * CONV_ROWS:c * CONV_ROWS + CONV_ROWS + CONV_HIST, cs]
    acc = jnp.broadcast_to(dwb_ref[:, cs], (CONV_ROWS, LANES))
    for phase in range(SUBLANES):
        taps = [k for k in range(CONV_WIDTH) if (-(first + k)) % SUBLANES == phase]
        if not taps:
            continue
        wb = win if phase == 0 else pltpu.roll(win, phase, axis=0)
        for k in taps:
            i0 = first + k + phase
            acc = acc + dww_ref[k:k + 1, cs] * wb[i0:i0 + CONV_ROWS, :]
    return acc


def _mlstm(zcol, w, c_st, n_st, m_st, row_valid, want_h):
    t = SEQ_TILE
    zif = zcol(C_IF, N_PROJ)
    ig_all = zif
    lf_all = jax.nn.log_sigmoid(zif)
    if row_valid is not None:
        ig_all = jnp.where(row_valid, ig_all, NEG_GATE)
        lf_all = jnp.where(row_valid, lf_all, 0.0)
    rows = lax.broadcasted_iota(jnp.int32, (t, t), 0)
    cols = lax.broadcasted_iota(jnp.int32, (t, t), 1)
    causal = cols <= rows
    lf_terms = jnp.concatenate(_split_bf16(lf_all, 3), axis=1)
    b3 = jnp.dot(causal.astype(BF16), lf_terms, preferred_element_type=F32)
    b_all = b3[:, 0:LANES] + b3[:, LANES:2 * LANES] + b3[:, 2 * LANES:3 * LANES]
    b_sh = pltpu.roll(b_all, LANES - M_HEADS, axis=1)
    a_all = ig_all - b_sh
    a_rows = a_all.T
    hs = []
    for h in range(M_HEADS):
        a_row = a_rows[h:h + 1, :]
        a_col = a_all[:, h:h + 1]
        b_col = b_sh[:, h:h + 1]
        m_prev = m_st[h][0:1, 0:1]
        n_prev = n_st[h][0:1, :]
        ct = c_st[h]
        k = zcol(C_K + h * M_DK, C_K + (h + 1) * M_DK)
        v_bf = zcol(C_V + h * M_DV, C_V + (h + 1) * M_DV).astype(BF16)
        amat = jnp.where(causal, a_row, -jnp.inf)
        m_run = jnp.maximum(jnp.max(amat, axis=-1, keepdims=True), m_prev)
        if want_h:
            q = zcol(C_Q + h * M_DK, C_Q + (h + 1) * M_DK) * QK_SCALE
            q_bf = q.astype(BF16)
            s = lax.dot_general(q_bf, k.astype(BF16), (((1,), (1,)), ((), ())),
                                preferred_element_type=F32)
            p = s * jnp.exp(amat - m_run)
            inter_sc = jnp.exp(m_prev - m_run)
            num = (jnp.dot(p.astype(BF16), v_bf, preferred_element_type=F32)
                   + inter_sc * jnp.dot(q_bf, ct.astype(BF16), preferred_element_type=F32))
            den = (jnp.sum(p, axis=-1, keepdims=True)
                   + inter_sc * jnp.sum(q * n_prev, axis=-1, keepdims=True))
            scale = 1.0 / jnp.maximum(jnp.abs(den), jnp.exp(-(b_col + m_run)))
            hh = num * scale
            hh = hh * lax.rsqrt(jnp.mean(hh * hh, axis=-1, keepdims=True) + NORM_EPS)
            hs.append(hh * w["mng"][:, h * M_DV:(h + 1) * M_DV])
        m_last = m_run[t - 1:t, :]
        decay = jnp.exp(m_prev - m_last)
        kw = k * jnp.exp(a_col - m_last)
        c_st[h] = decay * ct + jnp.dot(kw.T.astype(BF16), v_bf, preferred_element_type=F32)
        n_new = decay * n_prev + jnp.sum(kw, axis=0, keepdims=True)
        n_st[h] = jnp.broadcast_to(n_new, (SUBLANES, M_DK))
        m_st[h] = jnp.broadcast_to(b_col[t - 1:t, :] + m_last, (SUBLANES, LANES))
    return jnp.concatenate(hs, axis=1) if want_h else None


def _router(u2, rw_ref, rb_ref):
    t = u2.shape[0]
    u_hi, u_mid = _split_bf16(u2, 2)
    lhs = jnp.concatenate([u_hi, u_mid, u_hi], axis=1)
    logits_t = lax.dot_general(rw_ref[...], lhs, (((1,), (1,)), ((), ())),
                               preferred_element_type=F32)
    cur = logits_t[0:N_EXPERTS, :] + rb_ref[:, 0:1]
    row = lax.broadcasted_iota(jnp.int32, (N_EXPERTS, t), 0)
    idx_rows = []
    exp_rows = []
    top0 = None
    esum = None
    for kk in range(TOP_K):
        mx = jnp.max(cur, axis=0, keepdims=True)
        ix = jnp.min(jnp.where(cur == mx, row, N_EXPERTS), axis=0, keepdims=True)
        if kk == 0:
            top0 = mx
        ek = jnp.exp(mx - top0)
        esum = ek if kk == 0 else esum + ek
        idx_rows.append(ix)
        exp_rows.append(ek)
        cur = jnp.where(row == ix, -jnp.inf, cur)
    idx_t = jnp.concatenate(idx_rows + [jnp.zeros((SUBLANES - TOP_K, t), jnp.int32)], axis=0)
    gates_t = jnp.concatenate([r / esum for r in exp_rows]
                              + [jnp.zeros((LANES - TOP_K, t), F32)], axis=0)
    return idx_t, gates_t.T


def _pack_rows(u2, out_ref):
    t = u2.shape[0]
    bits = lax.bitcast_convert_type(u2.astype(BF16).astype(F32), U32)
    for i in range(PACK_ROWS):
        lo = bits[:, i * 2 * LANES:i * 2 * LANES + LANES]
        hi = bits[:, i * 2 * LANES + LANES:(i + 1) * 2 * LANES]
        out_ref[pl.ds(i, t, stride=PACK_ROWS), :] = hi | (lo >> 16)


def _unpack_rows(tile_ref, slot, rows):
    parts = []
    for i in range(PACK_ROWS):
        word = tile_ref.at[slot][pl.ds(i, rows, stride=PACK_ROWS), :]
        lo = lax.bitcast_convert_type(word << 16, F32)
        hi = lax.bitcast_convert_type(word & jnp.uint32(0xFFFF0000), F32)
        parts += [lo.astype(BF16), hi.astype(BF16)]
    return jnp.concatenate(parts, axis=1)


def _weights(refs):
    names = ("g1", "win", "bin", "dww", "dwb", "lng", "lnb", "pww", "pwb", "mng", "mow",
             "wout", "g2", "rw", "rb")
    return dict(zip(names, refs))


N_WEIGHTS = 15


def _prefix_kernel(*refs):
    x_ref = refs[0]
    w = _weights(refs[1:1 + N_WEIGHTS])
    ytail_ref, c_out, n_out, m_out = refs[1 + N_WEIGHTS:]
    t = SEQ_TILE
    row_valid = lax.broadcasted_iota(jnp.int32, (t, 1), 0) >= t - N_META
    c_out[...] = jnp.zeros(c_out.shape, F32)
    n_out[...] = jnp.zeros(n_out.shape, F32)
    m_out[...] = jnp.zeros(m_out.shape, F32)
    u_bf = _rms_norm(x_ref[...], w["g1"][...]).astype(BF16)
    y = _glu(u_bf, w, row_valid)
    ytail_ref[...] = y[t - CONV_HIST:, :]
    _mlstm(lambda lo, hi: _proj(u_bf, w, lo, hi), w, c_out, n_out, m_out, row_valid, want_h=False)


def _mixer_kernel(*refs):
    x_ref = refs[0]
    w = _weights(refs[1:1 + N_WEIGHTS])
    y0_ref, c0_ref, n0_ref, m0_ref = refs[1 + N_WEIGHTS:5 + N_WEIGHTS]
    h1_ref, u2p_ref, idx_ref, gate_ref = refs[5 + N_WEIGHTS:9 + N_WEIGHTS]
    ybuf, cbuf, zbuf, c_st, n_st, m_st = refs[9 + N_WEIGHTS:]
    t = SEQ_TILE

    @pl.when(pl.program_id(1) == 0)
    def _():
        ybuf[0:CONV_HIST, :] = y0_ref[...]
        c_st[...] = c0_ref[...]
        n_st[...] = n0_ref[...]
        m_st[...] = m0_ref[...]

    x = x_ref[0]
    u_bf = _rms_norm(x, w["g1"][...]).astype(BF16)

    ybuf[CONV_HIST:CONV_HIST + t, :] = _glu(u_bf, w, None)
    chunks = [(lo, min(lo + PROJ_CHUNK, N_PROJ)) for lo in range(C_Q, N_PROJ, PROJ_CHUNK)]
    blocks = [(c, j) for c in range(t // CONV_ROWS) for j in range(D_MODEL // LANES)]
    assert len(chunks) <= len(blocks)
    for n, (c, j) in enumerate(blocks):
        cbuf[c * CONV_ROWS:(c + 1) * CONV_ROWS, j * LANES:(j + 1) * LANES] = _conv_block(
            ybuf, w["dww"], w["dwb"], c, j)
        if n < len(chunks):
            lo, hi = chunks[n]
            zbuf[:, lo - C_Q:hi - C_Q] = _proj(u_bf, w, lo, hi)
    ybuf[0:CONV_HIST, :] = ybuf[t:t + CONV_HIST, :]
    zcol = lambda lo, hi: zbuf[:, lo - C_Q:hi - C_Q]

    conv = cbuf[...]
    mu = jnp.mean(conv, axis=-1, keepdims=True)
    cen = conv - mu
    var = jnp.mean(cen * cen, axis=-1, keepdims=True)
    ln = cen * lax.rsqrt(var + NORM_EPS) * w["lng"][...] + w["lnb"][...]
    conv_out = jnp.dot((ln * _sigmoid(ln)).astype(BF16), w["pww"][...],
                       preferred_element_type=F32) + w["pwb"][...]

    hcat = _mlstm(zcol, w, c_st, n_st, m_st, None, want_h=True)
    o_gate = _sigmoid(zcol(C_O, C_GC))
    mlstm_out = jnp.dot((o_gate * hcat).astype(BF16), w["mow"][...], preferred_element_type=F32)

    g_conv = _sigmoid(zcol(C_GC, C_GM))
    g_mlstm = _sigmoid(zcol(C_GM, C_IF))
    mix = (g_conv * conv_out + g_mlstm * mlstm_out).astype(BF16)
    h1 = x + jnp.dot(mix, w["wout"][...], preferred_element_type=F32)
    h1_ref[0] = h1

    u2 = _rms_norm(h1, w["g2"][...])
    _pack_rows(u2, u2p_ref)
    idx, gates = _router(u2, w["rw"], w["rb"])
    idx_ref[...] = idx
    gate_ref[0] = gates


def _const_spec(shape):
    nd = len(shape)
    return pl.BlockSpec(shape, lambda *_: (0,) * nd, pipeline_mode=pl.Buffered(1))


def _mixer_weights(norm_mix_g, w_in, b_in, conv_dw_w, conv_dw_b, conv_ln_g, conv_ln_b, conv_pw_w,
                   conv_pw_b, mlstm_norm_g, mlstm_out_w, w_out, norm_ffn_g, router_w, router_b):
    d = D_MODEL
    qo, ko, vo = 2 * d, 2 * d + 512, 2 * d + 1024
    io = vo + d
    oo = io + 2 * M_HEADS

    def regroup(a):
        parts = [a[..., 0:qo], a[..., qo:ko], a[..., ko:vo], a[..., vo:io],
                 a[..., oo:oo + d], a[..., oo + d:oo + 2 * d], a[..., oo + 2 * d:oo + 3 * d],
                 a[..., io:oo],
                 jnp.zeros(a.shape[:-1] + (LANES - 2 * M_HEADS,), a.dtype)]
        return jnp.concatenate(parts, axis=-1)

    row = lambda a: a.reshape(1, -1).astype(F32)
    dww = jnp.pad(conv_dw_w[0].astype(F32), ((0, 32 - CONV_WIDTH), (0, 0)))
    rw = jnp.pad(router_w[0].astype(F32), ((0, 0), (0, LANES - N_EXPERTS)))
    rw_hi = rw.astype(BF16)
    rw_mid = (rw - rw_hi.astype(F32)).astype(BF16)
    rb = jnp.broadcast_to(router_b[0].astype(F32)[:, None], (N_EXPERTS, LANES))
    return [row(norm_mix_g[0]), regroup(w_in[0].astype(BF16)), row(regroup(b_in[0])),
            dww, row(conv_dw_b[0]), row(conv_ln_g[0]), row(conv_ln_b[0]),
            conv_pw_w[0].astype(BF16), row(conv_pw_b[0]), row(mlstm_norm_g[0]),
            mlstm_out_w[0].astype(BF16), w_out[0].astype(BF16), row(norm_ffn_g[0]),
            jnp.concatenate([rw_hi, rw_hi, rw_mid], axis=0).T, rb]


def _mixer(x, meta_tokens, weights):
    bsz, seq, d = x.shape
    t = SEQ_TILE
    n_seq = seq // t
    w_specs = [_const_spec(a.shape) for a in weights]
    state_shapes = [(CONV_HIST, d), (M_HEADS, M_DK, M_DV), (M_HEADS, SUBLANES, M_DK),
                    (M_HEADS, SUBLANES, LANES)]

    x_meta = jnp.concatenate([jnp.zeros((t - N_META, d), F32), meta_tokens.astype(F32)], axis=0)
    state = pl.pallas_call(
        _prefix_kernel,
        grid=(1,),
        in_specs=[pl.BlockSpec((t, d), lambda i: (0, 0))] + w_specs,
        out_specs=[pl.BlockSpec(s, lambda i, n=len(s): (0,) * n) for s in state_shapes],
        out_shape=[jax.ShapeDtypeStruct(s, F32) for s in state_shapes],
        compiler_params=pltpu.CompilerParams(vmem_limit_bytes=VMEM_LIMIT),
        name="prefix",
    )(x_meta, *weights)

    tile = lambda b, s: (b, s, 0)
    return pl.pallas_call(
        _mixer_kernel,
        grid=(bsz, n_seq),
        in_specs=([pl.BlockSpec((1, t, d), tile)] + w_specs
                  + [_const_spec(s) for s in state_shapes]),
        out_specs=[pl.BlockSpec((1, t, d), tile),
                   pl.BlockSpec((t * PACK_ROWS, LANES), lambda b, s: (b * n_seq + s, 0)),
                   pl.BlockSpec((SUBLANES, t), lambda b, s: (0, b * n_seq + s)),
                   pl.BlockSpec((1, t, LANES), tile)],
        out_shape=[jax.ShapeDtypeStruct((bsz, seq, d), F32),
                   jax.ShapeDtypeStruct((bsz * seq * PACK_ROWS, LANES), U32),
                   jax.ShapeDtypeStruct((SUBLANES, bsz * seq), jnp.int32),
                   jax.ShapeDtypeStruct((bsz, seq, LANES), F32)],
        scratch_shapes=[pltpu.VMEM((CONV_HIST + t, d), F32),
                        pltpu.VMEM((t, d), F32),
                        pltpu.VMEM((t, N_PROJ - C_Q), F32),
                        pltpu.VMEM((M_HEADS, M_DK, M_DV), F32),
                        pltpu.VMEM((M_HEADS, SUBLANES, M_DK), F32),
                        pltpu.VMEM((M_HEADS, SUBLANES, LANES), F32)],
        compiler_params=pltpu.CompilerParams(vmem_limit_bytes=VMEM_LIMIT,
                                             dimension_semantics=("arbitrary", "arbitrary")),
        name="mixer",
    )(x, *weights, *state)


def _chunk_rows(c):
    start = c * W_CHUNK
    return pl.ds(start if isinstance(c, int) else pl.multiple_of(start, W_CHUNK), W_CHUNK)


def _expert_kernel(cnt_ref, start_ref, total_ref, tok_ref, row_ref,
                   x_ref, w1_hbm, b1_ref, w2_hbm, b2_ref, ys_hbm,
                   tile, ybuf, wb1, wb2, st1, st2, osem, wsem1, wsem2):
    e = pl.program_id(0)
    n_exp = pl.num_programs(0)
    nb = cnt_ref[e]
    b0 = start_ref[e]
    total = total_ref[0]
    bm = MOE_BM
    n_planes_rows = ys_hbm.shape[0] - (N_EXPERTS + 1) * bm
    spare_block = n_planes_rows + N_EXPERTS * bm
    wcur = e % 2
    n_chunks = D_MODEL // W_CHUNK

    def chunk_copies(ex, c):
        rows = _chunk_rows(c)
        s = c % 2
        return (pltpu.make_async_copy(w1_hbm.at[ex, rows, :], st1.at[s], wsem1.at[s]),
                pltpu.make_async_copy(w2_hbm.at[ex, rows, :], st2.at[s], wsem2.at[s]))

    def chunk_start(ex, c):
        for cp in chunk_copies(ex, c):
            cp.start()

    def chunk_finish(ex, c, wslot):
        for cp in chunk_copies(ex, c):
            cp.wait()
        rows = _chunk_rows(c)
        wb1[wslot, rows, :] = st1[c % 2].astype(BF16)
        wb2[wslot, rows, :] = st2[c % 2].astype(BF16)

    def stream_next(c):
        c = jnp.asarray(c, jnp.int32)

        @pl.when(jnp.logical_and(c < n_chunks, e + 1 < n_exp))
        def _():
            chunk_finish(e + 1, c, 1 - wcur)

            @pl.when(c + 2 < n_chunks)
            def _():
                chunk_start(e + 1, c + 2)

    def gather(blk, slot):
        for r in range(bm):
            src = pl.multiple_of(tok_ref[blk * bm + r], PACK_ROWS)
            tile[slot, pl.ds(PACK_ROWS * r, PACK_ROWS), :] = x_ref[pl.ds(src, PACK_ROWS), :]

    def scatter_start(blk, slot):
        for r in range(bm):
            dst = row_ref[blk * bm + r]
            pltpu.make_async_copy(ybuf.at[slot, pl.ds(r, 1), :], ys_hbm.at[pl.ds(dst, 1), :],
                                  osem.at[slot]).start(priority=r % 2)

    def block_copy(slot, row0):
        return pltpu.make_async_copy(ybuf.at[slot], ys_hbm.at[pl.ds(row0, bm), :], osem.at[slot])

    def scatter_wait(slot):
        block_copy(slot, 0).wait()

    def mlp(slot):
        xb = _unpack_rows(tile, slot, bm)
        hcat = jnp.dot(xb, wb1[wcur], preferred_element_type=F32) + b1_ref[0]
        h_glu = jnp.minimum(hcat[:, :D_FF], SWIGLU_LIMIT)
        h_lin = jnp.clip(hcat[:, D_FF:], -SWIGLU_LIMIT, SWIGLU_LIMIT)
        act = h_glu * _sigmoid(SWIGLU_ALPHA * h_glu) * (h_lin + 1.0)
        y = jnp.dot(act.astype(BF16), wb2[wcur], preferred_element_type=F32) + b2_ref[0]
        bits = lax.bitcast_convert_type(y.astype(BF16).astype(F32), U32)
        return bits[:, D_MODEL // PACK:] | (bits[:, :D_MODEL // PACK] >> 16)

    @pl.when(e == 0)
    def _():
        chunk_start(0, 0)
        chunk_start(0, 1)
        for c in range(n_chunks):
            chunk_finish(0, c, 0)
            if c + 2 < n_chunks:
                chunk_start(0, c + 2)
        ybuf[...] = jnp.zeros(ybuf.shape, U32)
        for i in range(N_EXPERTS):
            block_copy(0, n_planes_rows + i * bm).start()
        for i in range(N_EXPERTS):
            block_copy(0, n_planes_rows + i * bm).wait()

    @pl.when(e + 1 < n_exp)
    def _():
        chunk_start(e + 1, 0)
        chunk_start(e + 1, 1)

    @pl.when(nb > 0)
    def _():
        @pl.when(b0 == 0)
        def _():
            gather(0, 0)
            y = mlp(0)
            gather(1, 1)
            ybuf[0] = y
            block_copy(1, spare_block).start()
            stream_next(0)

        def block_body(blk, slot):
            y = mlp(slot)
            gather(blk + 1, 1 - slot)
            scatter_start(blk - 1, 1 - slot)
            scatter_wait(slot)
            ybuf[slot] = y

        def block(i, carry):
            blk = b0 + i
            for slot in range(2):
                @pl.when(blk % 2 == slot)
                def _():
                    block_body(blk, slot)
            stream_next(i)
            return carry

        lax.fori_loop(jnp.where(b0 == 0, 1, 0), nb, block, 0)

    lax.fori_loop(jnp.minimum(nb, n_chunks), n_chunks, lambda c, carry: (stream_next(c), carry)[1], 0)

    @pl.when(e == pl.num_programs(0) - 1)
    def _():
        for last in range(2):
            @pl.when((total - 1) % 2 == last)
            def _():
                scatter_start(total - 1, last)
        scatter_wait(0)
        scatter_wait(1)


def _experts(u2p, blk_cnt, blk_start, blk_total, slot_tok, slot_row, n_tok, w1, b1, w2, b2):
    d = D_MODEL
    emap = lambda e, *_: (e, 0, 0)
    n_rows = TOP_K * n_tok + (N_EXPERTS + 1) * MOE_BM
    grid_spec = pltpu.PrefetchScalarGridSpec(
        num_scalar_prefetch=5,
        grid=(N_EXPERTS,),
        in_specs=[pl.BlockSpec(u2p.shape, lambda e, *_: (0, 0), pipeline_mode=pl.Buffered(1)),
                  pl.BlockSpec(memory_space=pl.ANY),
                  pl.BlockSpec((1, 1, 2 * D_FF), emap),
                  pl.BlockSpec(memory_space=pl.ANY),
                  pl.BlockSpec((1, 1, d), emap)],
        out_specs=pl.BlockSpec(memory_space=pl.ANY),
        scratch_shapes=[pltpu.VMEM((2, MOE_BM * PACK_ROWS, LANES), U32),
                        pltpu.VMEM((2, MOE_BM, d // PACK), U32),
                        pltpu.VMEM((2, d, 2 * D_FF), BF16),
                        pltpu.VMEM((2, D_FF, d), BF16),
                        pltpu.VMEM((2, W_CHUNK, 2 * D_FF), F32),
                        pltpu.VMEM((2, W_CHUNK, d), F32),
                        pltpu.SemaphoreType.DMA((2,)),
                        pltpu.SemaphoreType.DMA((2,)),
                        pltpu.SemaphoreType.DMA((2,))],
    )
    return pl.pallas_call(
        _expert_kernel,
        grid_spec=grid_spec,
        out_shape=jax.ShapeDtypeStruct((n_rows, d // PACK), U32),
        compiler_params=pltpu.CompilerParams(vmem_limit_bytes=VMEM_LIMIT,
                                             dimension_semantics=("arbitrary",)),
        name="experts",
    )(blk_cnt, blk_start, blk_total, slot_tok, slot_row, u2p,
      w1.astype(F32), b1.reshape(N_EXPERTS, 1, 2 * D_FF), w2.astype(F32),
      b2.reshape(N_EXPERTS, 1, d))


def _combine_kernel(h1_ref, gate_ref, g_ref, y0_ref, y1_ref, y2_ref, y3_ref, out_ref):
    acc = h1_ref[...]
    gates = gate_ref[...]
    for kk, y_ref in enumerate((y0_ref, y1_ref, y2_ref, y3_ref)):
        word = y_ref[...]
        lo = lax.bitcast_convert_type(word << 16, F32)
        hi = lax.bitcast_convert_type(word & jnp.uint32(0xFFFF0000), F32)
        acc = acc + gates[:, kk:kk + 1] * jnp.concatenate([lo, hi], axis=1)
    out_ref[...] = _rms_norm(acc, g_ref[...])


def _combine(h1, gates, final_g, ys):
    n, d = h1.shape
    tt = COMB_TILE
    nt = n // tt
    plane = lambda kk: pl.BlockSpec((tt, d // PACK), lambda i, kk=kk: (kk * nt + i, 0))
    return pl.pallas_call(
        _combine_kernel,
        grid=(nt,),
        in_specs=[pl.BlockSpec((tt, d), lambda i: (i, 0)),
                  pl.BlockSpec((tt, LANES), lambda i: (i, 0)),
                  pl.BlockSpec((1, d), lambda i: (0, 0))] + [plane(kk) for kk in range(TOP_K)],
        out_specs=pl.BlockSpec((tt, d), lambda i: (i, 0)),
        out_shape=jax.ShapeDtypeStruct((n, d), F32),
        compiler_params=pltpu.CompilerParams(vmem_limit_bytes=VMEM_LIMIT,
                                             dimension_semantics=("arbitrary",)),
        name="combine",
    )(h1, gates, final_g.reshape(1, d).astype(F32), ys, ys, ys, ys)


def _routing_tables(top_idx):
    n_tok = top_idx.shape[1]
    n_asg = n_tok * TOP_K
    bm = MOE_BM
    flat_e = top_idx.reshape(-1).astype(jnp.int32)
    sorted_e, order = lax.sort((flat_e, jnp.arange(n_asg, dtype=jnp.int32)), num_keys=1)
    experts = jnp.arange(N_EXPERTS, dtype=jnp.int32)
    counts = jnp.sum((flat_e[:, None] == experts[None, :]).astype(jnp.int32), axis=0)
    grp_start = jnp.cumsum(counts).astype(jnp.int32) - counts
    blk_cnt = (counts + bm - 1) // bm
    blk_end = jnp.cumsum(blk_cnt).astype(jnp.int32)
    blk_start = blk_end - blk_cnt
    n_blocks = n_asg // bm + N_EXPERTS
    blk = jnp.arange(n_blocks, dtype=jnp.int32)
    blk_e = jnp.minimum(jnp.sum((blk[:, None] >= blk_end[None, :]).astype(jnp.int32), axis=1),
                        N_EXPERTS - 1)
    e_count = counts[blk_e][:, None]
    e_first = grp_start[blk_e][:, None]
    within = ((blk - blk_start[blk_e]) * bm)[:, None] + jnp.arange(bm, dtype=jnp.int32)[None, :]
    real = within < e_count
    asg = order[jnp.where(real, e_first + within, 0)]
    tok = asg % n_tok
    pad_ord = blk[:, None] * bm + jnp.arange(bm, dtype=jnp.int32)[None, :] - (
        e_first + jnp.minimum(within, e_count))
    slot_tok = jnp.where(real, tok * PACK_ROWS, 0).astype(jnp.int32).reshape(-1)
    slot_row = jnp.where(real, asg,
                         n_asg + pad_ord).astype(jnp.int32).reshape(-1)
    return blk_cnt, blk_start, blk_end[-1:], slot_tok, slot_row


def kernel(x, meta_tokens, norm_mix_g, w_in, b_in, conv_dw_w, conv_dw_b, conv_ln_g, conv_ln_b, conv_pw_w, conv_pw_b, mlstm_norm_g, mlstm_out_w, w_out, norm_ffn_g, router_w, router_b, expert_w1, expert_b1, expert_w2, expert_b2, final_norm_g):
    bsz, seq, d = x.shape
    n_tok = bsz * seq
    assert d == D_MODEL and seq % SEQ_TILE == 0 and n_tok % COMB_TILE == 0 and w_in.shape[0] == 1
    weights = _mixer_weights(norm_mix_g, w_in, b_in, conv_dw_w, conv_dw_b, conv_ln_g, conv_ln_b,
                             conv_pw_w, conv_pw_b, mlstm_norm_g, mlstm_out_w, w_out, norm_ffn_g,
                             router_w, router_b)
    h1, u2p, idx, gates = _mixer(x.astype(F32), meta_tokens, weights)
    h1 = h1.reshape(n_tok, d)
    gates = gates.reshape(n_tok, LANES)
    top_idx = idx[:TOP_K]
    blk_cnt, blk_start, blk_total, slot_tok, slot_row = _routing_tables(top_idx)
    ys = _experts(u2p, blk_cnt, blk_start, blk_total, slot_tok, slot_row, n_tok,
                  expert_w1[0], expert_b1[0], expert_w2[0], expert_b2[0])
    out = _combine(h1, gates, final_norm_g, ys)
    return out.reshape(bsz, seq, d)
```

```python
import jax
import jax.numpy as jnp
from jax import lax
from jax.experimental import pallas as pl
from jax.experimental.pallas import tpu as pltpu

F32 = jnp.float32
BF16 = jnp.bfloat16
U32 = jnp.uint32

D_MODEL = 1024
N_META = 16
CONV_WIDTH = 31
M_HEADS = 4
M_DK = 128
M_DV = 256
QK_SCALE = M_DK ** -0.5
N_EXPERTS = 32
TOP_K = 4
D_FF = D_MODEL
SWIGLU_ALPHA = 1.702
SWIGLU_LIMIT = 7.0
NORM_EPS = 1e-5
NEG_GATE = -1.0e4

LANES = 128
SUBLANES = 8
VMEM_LIMIT = 56 * 1024 * 1024

SEQ_TILE = 256
CONV_HIST = 32
CONV_ROWS = 64
PROJ_CHUNK = 256
MOE_BM = 128
W_CHUNK = 128
COMB_TILE = 1024
PACK = 2
PACK_ROWS = D_MODEL // (PACK * LANES)

C_GLU = 0
C_Q = C_GLU + 2 * D_MODEL
C_K = C_Q + M_HEADS * M_DK
C_V = C_K + M_HEADS * M_DK
C_IF = C_V + M_HEADS * M_DV
C_O = C_IF + LANES
C_GC = C_O + D_MODEL
C_GM = C_GC + D_MODEL
N_PROJ = C_GM + D_MODEL


def _rms_norm(x, g):
    return x * lax.rsqrt(jnp.mean(x * x, axis=-1, keepdims=True) + NORM_EPS) * g


def _sigmoid(x):
    return 0.5 * jnp.tanh(0.5 * x) + 0.5


def _split_bf16(x, terms):
    parts = []
    for _ in range(terms - 1):
        p = x.astype(BF16)
        parts.append(p)
        x = x - p.astype(F32)
    parts.append(x.astype(BF16))
    return parts


def _proj(u_bf, w, lo, hi):
    return jnp.dot(u_bf, w["win"][:, lo:hi], preferred_element_type=F32) + w["bin"][:, lo:hi]


def _glu(u_bf, w, row_valid):
    zg = _proj(u_bf, w, C_GLU, C_Q)
    y = zg[:, :D_MODEL] * _sigmoid(zg[:, D_MODEL:])
    if row_valid is not None:
        y = jnp.where(row_valid, y, 0.0)
    return y


def _conv_block(ybuf, dww_ref, dwb_ref, c, j):
    first = CONV_HIST - (CONV_WIDTH - 1)
    cs = slice(j * LANES, (j + 1) * LANES)
    win = ybuf[c * CONV_ROWS:c * CONV_ROWS + CONV_ROWS + CONV_HIST, cs]
    acc = jnp.broadcast_to(dwb_ref[:, cs], (CONV_ROWS, LANES))
    for phase in range(SUBLANES):
        taps = [k for k in range(CONV_WIDTH) if (-(first + k)) % SUBLANES == phase]
        if not taps:
            continue
        wb = win if phase == 0 else pltpu.roll(win, phase, axis=0)
        for k in taps:
            i0 = first + k + phase
            acc = acc + dww_ref[k:k + 1, cs] * wb[i0:i0 + CONV_ROWS, :]
    return acc


def _mlstm(zcol, w, c_st, n_st, m_st, row_valid, want_h):
    t = SEQ_TILE
    zif = zcol(C_IF, C_O)
    ig_all = zif
    lf_all = jax.nn.log_sigmoid(zif)
    if row_valid is not None:
        ig_all = jnp.where(row_valid, ig_all, NEG_GATE)
        lf_all = jnp.where(row_valid, lf_all, 0.0)
    rows = lax.broadcasted_iota(jnp.int32, (t, t), 0)
    cols = lax.broadcasted_iota(jnp.int32, (t, t), 1)
    causal = cols <= rows
    lf_terms = jnp.concatenate(_split_bf16(lf_all, 3), axis=1)
    b3 = jnp.dot(causal.astype(BF16), lf_terms, preferred_element_type=F32)
    b_all = b3[:, 0:LANES] + b3[:, LANES:2 * LANES] + b3[:, 2 * LANES:3 * LANES]
    b_sh = pltpu.roll(b_all, LANES - M_HEADS, axis=1)
    a_all = ig_all - b_sh
    a_rows = a_all.T
    hs = []
    for h in range(M_HEADS):
        a_row = a_rows[h:h + 1, :]
        a_col = a_all[:, h:h + 1]
        b_col = b_sh[:, h:h + 1]
        m_prev = m_st[h][0:1, 0:1]
        n_prev = n_st[h][0:1, :]
        ct = c_st[h]
        k = zcol(C_K + h * M_DK, C_K + (h + 1) * M_DK)
        v_bf = zcol(C_V + h * M_DV, C_V + (h + 1) * M_DV).astype(BF16)
        amat = jnp.where(causal, a_row, -jnp.inf)
        m_run = jnp.maximum(jnp.max(amat, axis=-1, keepdims=True), m_prev)
        if want_h:
            q = zcol(C_Q + h * M_DK, C_Q + (h + 1) * M_DK) * QK_SCALE
            q_bf = q.astype(BF16)
            s = lax.dot_general(q_bf, k.astype(BF16), (((1,), (1,)), ((), ())),
                                preferred_element_type=F32)
            p = s * jnp.exp(amat - m_run)
            inter_sc = jnp.exp(m_prev - m_run)
            num = (jnp.dot(p.astype(BF16), v_bf, preferred_element_type=F32)
                   + inter_sc * jnp.dot(q_bf, ct.astype(BF16), preferred_element_type=F32))
            den = (jnp.sum(p, axis=-1, keepdims=True)
                   + inter_sc * jnp.sum(q * n_prev, axis=-1, keepdims=True))
            scale = 1.0 / jnp.maximum(jnp.abs(den), jnp.exp(-(b_col + m_run)))
            hh = num * scale
            hh = hh * lax.rsqrt(jnp.mean(hh * hh, axis=-1, keepdims=True) + NORM_EPS)
            hs.append(hh * w["mng"][:, h * M_DV:(h + 1) * M_DV])
        m_last = m_run[t - 1:t, :]
        decay = jnp.exp(m_prev - m_last)
        kw = k * jnp.exp(a_col - m_last)
        c_st[h] = decay * ct + jnp.dot(kw.T.astype(BF16), v_bf, preferred_element_type=F32)
        n_new = decay * n_prev + jnp.sum(kw, axis=0, keepdims=True)
        n_st[h] = jnp.broadcast_to(n_new, (SUBLANES, M_DK))
        m_st[h] = jnp.broadcast_to(b_col[t - 1:t, :] + m_last, (SUBLANES, LANES))
    return jnp.concatenate(hs, axis=1) if want_h else None


def _router(u2, rw_ref, rb_ref):
    t = u2.shape[0]
    u_hi, u_mid = _split_bf16(u2, 2)
    lhs = jnp.concatenate([u_hi, u_mid, u_hi], axis=1)
    logits_t = lax.dot_general(rw_ref[...], lhs, (((1,), (1,)), ((), ())),
                               preferred_element_type=F32)
    cur = logits_t[0:N_EXPERTS, :] + rb_ref[:, 0:1]
    row = lax.broadcasted_iota(jnp.int32, (N_EXPERTS, t), 0)
    idx_rows = []
    exp_rows = []
    top0 = None
    esum = None
    for kk in range(TOP_K):
        mx = jnp.max(cur, axis=0, keepdims=True)
        ix = jnp.min(jnp.where(cur == mx, row, N_EXPERTS), axis=0, keepdims=True)
        if kk == 0:
            top0 = mx
        ek = jnp.exp(mx - top0)
        esum = ek if kk == 0 else esum + ek
        idx_rows.append(ix)
        exp_rows.append(ek)
        cur = jnp.where(row == ix, -jnp.inf, cur)
    idx_t = jnp.concatenate(idx_rows + [jnp.zeros((SUBLANES - TOP_K, t), jnp.int32)], axis=0)
    gates_t = jnp.concatenate([r / esum for r in exp_rows]
                              + [jnp.zeros((LANES - TOP_K, t), F32)], axis=0)
    return idx_t, gates_t.T


def _pack_rows(u2, out_ref):
    t = u2.shape[0]
    bits = lax.bitcast_convert_type(u2.astype(BF16).astype(F32), U32)
    for i in range(PACK_ROWS):
        lo = bits[:, i * 2 * LANES:i * 2 * LANES + LANES]
        hi = bits[:, i * 2 * LANES + LANES:(i + 1) * 2 * LANES]
        out_ref[pl.ds(i, t, stride=PACK_ROWS), :] = hi | (lo >> 16)


def _unpack_rows(tile_ref, slot, rows):
    parts = []
    for i in range(PACK_ROWS):
        word = tile_ref.at[slot][pl.ds(i, rows, stride=PACK_ROWS), :]
        lo = lax.bitcast_convert_type(word << 16, F32)
        hi = lax.bitcast_convert_type(word & jnp.uint32(0xFFFF0000), F32)
        parts += [lo.astype(BF16), hi.astype(BF16)]
    return jnp.concatenate(parts, axis=1)


def _weights(refs):
    names = ("g1", "win", "bin", "dww", "dwb", "lng", "lnb", "pww", "pwb", "mng", "mow",
             "wout", "g2", "rw", "rb")
    return dict(zip(names, refs))


N_WEIGHTS = 15


def _prefix_kernel(*refs):
    x_ref = refs[0]
    w = _weights(refs[1:1 + N_WEIGHTS])
    ytail_ref, c_out, n_out, m_out = refs[1 + N_WEIGHTS:]
    t = SEQ_TILE
    row_valid = lax.broadcasted_iota(jnp.int32, (t, 1), 0) >= t - N_META
    c_out[...] = jnp.zeros(c_out.shape, F32)
    n_out[...] = jnp.zeros(n_out.shape, F32)
    m_out[...] = jnp.zeros(m_out.shape, F32)
    u_bf = _rms_norm(x_ref[...], w["g1"][...]).astype(BF16)
    y = _glu(u_bf, w, row_valid)
    ytail_ref[...] = y[t - CONV_HIST:, :]
    _mlstm(lambda lo, hi: _proj(u_bf, w, lo, hi), w, c_out, n_out, m_out, row_valid, want_h=False)


def _mixer_kernel(*refs):
    x_ref = refs[0]
    w = _weights(refs[1:1 + N_WEIGHTS])
    y0_ref, c0_ref, n0_ref, m0_ref = refs[1 + N_WEIGHTS:5 + N_WEIGHTS]
    h1_ref, u2p_ref, idx_ref, gate_ref = refs[5 + N_WEIGHTS:9 + N_WEIGHTS]
    ybuf, cbuf, zbuf, c_st, n_st, m_st = refs[9 + N_WEIGHTS:]
    t = SEQ_TILE

    @pl.when(pl.program_id(1) == 0)
    def _():
        ybuf[0:CONV_HIST, :] = y0_ref[...]
        c_st[...] = c0_ref[...]
        n_st[...] = n0_ref[...]
        m_st[...] = m0_ref[...]

    x = x_ref[0]
    u_bf = _rms_norm(x, w["g1"][...]).astype(BF16)

    ybuf[CONV_HIST:CONV_HIST + t, :] = _glu(u_bf, w, None)
    chunks = [(lo, min(lo + PROJ_CHUNK, N_PROJ)) for lo in range(C_Q, N_PROJ, PROJ_CHUNK)]
    blocks = [(c, j) for c in range(t // CONV_ROWS) for j in range(D_MODEL // LANES)]
    assert len(chunks) <= len(blocks)
    for n, (c, j) in enumerate(blocks):
        cbuf[c * CONV_ROWS:(c + 1) * CONV_ROWS, j * LANES:(j + 1) * LANES] = _conv_block(
            ybuf, w["dww"], w["dwb"], c, j)
        if n < len(chunks):
            lo, hi = chunks[n]
            zbuf[:, lo - C_Q:hi - C_Q] = jnp.dot(u_bf, w["win"][:, lo:hi],
                                                 preferred_element_type=F32)
    ybuf[0:CONV_HIST, :] = ybuf[t:t + CONV_HIST, :]
    zcol = lambda lo, hi: zbuf[:, lo - C_Q:hi - C_Q] + w["bin"][:, lo:hi]

    conv = cbuf[...]
    mu = jnp.mean(conv, axis=-1, keepdims=True)
    cen = conv - mu
    var = jnp.mean(cen * cen, axis=-1, keepdims=True)
    ln = cen * lax.rsqrt(var + NORM_EPS) * w["lng"][...] + w["lnb"][...]
    conv_out = jnp.dot((ln * _sigmoid(ln)).astype(BF16), w["pww"][...],
                       preferred_element_type=F32) + w["pwb"][...]

    hcat = _mlstm(zcol, w, c_st, n_st, m_st, None, want_h=True)
    o_gate = _sigmoid(zcol(C_O, C_GC))
    mlstm_out = jnp.dot((o_gate * hcat).astype(BF16), w["mow"][...], preferred_element_type=F32)

    g_conv = _sigmoid(zcol(C_GC, C_GM))
    g_mlstm = _sigmoid(zcol(C_GM, N_PROJ))
    mix = (g_conv * conv_out + g_mlstm * mlstm_out).astype(BF16)
    h1 = x + jnp.dot(mix, w["wout"][...], preferred_element_type=F32)
    h1_ref[0] = h1

    u2 = _rms_norm(h1, w["g2"][...])
    _pack_rows(u2, u2p_ref)
    idx, gates = _router(u2, w["rw"], w["rb"])
    idx_ref[...] = idx
    gate_ref[0] = gates


def _const_spec(shape):
    nd = len(shape)
    return pl.BlockSpec(shape, lambda *_: (0,) * nd, pipeline_mode=pl.Buffered(1))


def _mixer_weights(norm_mix_g, w_in, b_in, conv_dw_w, conv_dw_b, conv_ln_g, conv_ln_b, conv_pw_w,
                   conv_pw_b, mlstm_norm_g, mlstm_out_w, w_out, norm_ffn_g, router_w, router_b):
    d = D_MODEL
    qo, ko, vo = 2 * d, 2 * d + 512, 2 * d + 1024
    io = vo + d
    oo = io + 2 * M_HEADS

    def regroup(a):
        parts = [a[..., 0:oo], jnp.zeros(a.shape[:-1] + (LANES - 2 * M_HEADS,), a.dtype),
                 a[..., oo:]]
        return jnp.concatenate(parts, axis=-1)

    row = lambda a: a.reshape(1, -1).astype(F32)
    dww = jnp.pad(conv_dw_w[0].astype(F32), ((0, 32 - CONV_WIDTH), (0, 0)))
    rw = jnp.pad(router_w[0].astype(F32), ((0, 0), (0, LANES - N_EXPERTS)))
    rw_hi = rw.astype(BF16)
    rw_mid = (rw - rw_hi.astype(F32)).astype(BF16)
    rb = jnp.broadcast_to(router_b[0].astype(F32)[:, None], (N_EXPERTS, LANES))
    return [row(norm_mix_g[0]), regroup(w_in[0].astype(BF16)), row(regroup(b_in[0])),
            dww, row(conv_dw_b[0]), row(conv_ln_g[0]), row(conv_ln_b[0]),
            conv_pw_w[0].astype(BF16), row(conv_pw_b[0]), row(mlstm_norm_g[0]),
            mlstm_out_w[0].astype(BF16), w_out[0].astype(BF16), row(norm_ffn_g[0]),
            jnp.concatenate([rw_hi, rw_hi, rw_mid], axis=0).T, rb]


def _mixer(x, meta_tokens, weights):
    bsz, seq, d = x.shape
    t = SEQ_TILE
    n_seq = seq // t
    w_specs = [_const_spec(a.shape) for a in weights]
    state_shapes = [(CONV_HIST, d), (M_HEADS, M_DK, M_DV), (M_HEADS, SUBLANES, M_DK),
                    (M_HEADS, SUBLANES, LANES)]

    x_meta = jnp.concatenate([jnp.zeros((t - N_META, d), F32), meta_tokens.astype(F32)], axis=0)
    state = pl.pallas_call(
        _prefix_kernel,
        grid=(1,),
        in_specs=[pl.BlockSpec((t, d), lambda i: (0, 0))] + w_specs,
        out_specs=[pl.BlockSpec(s, lambda i, n=len(s): (0,) * n) for s in state_shapes],
        out_shape=[jax.ShapeDtypeStruct(s, F32) for s in state_shapes],
        compiler_params=pltpu.CompilerParams(vmem_limit_bytes=VMEM_LIMIT),
        name="prefix",
    )(x_meta, *weights)

    tile = lambda b, s: (b, s, 0)
    return pl.pallas_call(
        _mixer_kernel,
        grid=(bsz, n_seq),
        in_specs=([pl.BlockSpec((1, t, d), tile)] + w_specs
                  + [_const_spec(s) for s in state_shapes]),
        out_specs=[pl.BlockSpec((1, t, d), tile),
                   pl.BlockSpec((t * PACK_ROWS, LANES), lambda b, s: (b * n_seq + s, 0)),
                   pl.BlockSpec((SUBLANES, t), lambda b, s: (0, b * n_seq + s)),
                   pl.BlockSpec((1, t, LANES), tile)],
        out_shape=[jax.ShapeDtypeStruct((bsz, seq, d), F32),
                   jax.ShapeDtypeStruct((bsz * seq * PACK_ROWS, LANES), U32),
                   jax.ShapeDtypeStruct((SUBLANES, bsz * seq), jnp.int32),
                   jax.ShapeDtypeStruct((bsz, seq, LANES), F32)],
        scratch_shapes=[pltpu.VMEM((CONV_HIST + t, d), F32),
                        pltpu.VMEM((t, d), F32),
                        pltpu.VMEM((t, N_PROJ - C_Q), F32),
                        pltpu.VMEM((M_HEADS, M_DK, M_DV), F32),
                        pltpu.VMEM((M_HEADS, SUBLANES, M_DK), F32),
                        pltpu.VMEM((M_HEADS, SUBLANES, LANES), F32)],
        compiler_params=pltpu.CompilerParams(vmem_limit_bytes=VMEM_LIMIT,
                                             dimension_semantics=("arbitrary", "arbitrary")),
        name="mixer",
    )(x, *weights, *state)


def _chunk_rows(c):
    start = c * W_CHUNK
    return pl.ds(start if isinstance(c, int) else pl.multiple_of(start, W_CHUNK), W_CHUNK)


def _expert_kernel(cnt_ref, start_ref, total_ref, tok_ref, row_ref,
                   x_ref, w1_hbm, b1_ref, w2_hbm, b2_ref, ys_hbm,
                   tile, ybuf, wb1, wb2, st1, st2, osem, wsem1, wsem2):
    e = pl.program_id(0)
    n_exp = pl.num_programs(0)
    nb = cnt_ref[e]
    b0 = start_ref[e]
    total = total_ref[0]
    bm = MOE_BM
    n_planes_rows = ys_hbm.shape[0] - (N_EXPERTS + 1) * bm
    spare_block = n_planes_rows + N_EXPERTS * bm
    wcur = e % 2
    n_chunks = D_MODEL // W_CHUNK

    def chunk_copies(ex, c):
        rows = _chunk_rows(c)
        s = c % 2
        return (pltpu.make_async_copy(w1_hbm.at[ex, rows, :], st1.at[s], wsem1.at[s]),
                pltpu.make_async_copy(w2_hbm.at[ex, rows, :], st2.at[s], wsem2.at[s]))

    def chunk_start(ex, c):
        for cp in chunk_copies(ex, c):
            cp.start()

    def chunk_finish(ex, c, wslot):
        for cp in chunk_copies(ex, c):
            cp.wait()
        rows = _chunk_rows(c)
        wb1[wslot, rows, :] = st1[c % 2].astype(BF16)
        wb2[wslot, rows, :] = st2[c % 2].astype(BF16)

    def stream_next(c):
        c = jnp.asarray(c, jnp.int32)

        @pl.when(jnp.logical_and(c < n_chunks, e + 1 < n_exp))
        def _():
            chunk_finish(e + 1, c, 1 - wcur)

            @pl.when(c + 2 < n_chunks)
            def _():
                chunk_start(e + 1, c + 2)

    def gather(blk, slot):
        for r in range(bm):
            src = pl.multiple_of(tok_ref[blk * bm + r], PACK_ROWS)
            tile[slot, pl.ds(PACK_ROWS * r, PACK_ROWS), :] = x_ref[pl.ds(src, PACK_ROWS), :]

    def scatter_start(blk, slot):
        for r in range(bm):
            dst = row_ref[blk * bm + r]
            pltpu.make_async_copy(ybuf.at[slot, pl.ds(r, 1), :], ys_hbm.at[pl.ds(dst, 1), :],
                                  osem.at[slot]).start(priority=r % 2)

    def block_copy(slot, row0):
        return pltpu.make_async_copy(ybuf.at[slot], ys_hbm.at[pl.ds(row0, bm), :], osem.at[slot])

    def scatter_wait(slot):
        block_copy(slot, 0).wait()

    def mlp(slot):
        xb = _unpack_rows(tile, slot, bm)
        hcat = jnp.dot(xb, wb1[wcur], preferred_element_type=F32) + b1_ref[0]
        h_glu = jnp.minimum(hcat[:, :D_FF], SWIGLU_LIMIT)
        h_lin = jnp.clip(hcat[:, D_FF:], -SWIGLU_LIMIT, SWIGLU_LIMIT)
        act = h_glu * _sigmoid(SWIGLU_ALPHA * h_glu) * (h_lin + 1.0)
        y = jnp.dot(act.astype(BF16), wb2[wcur], preferred_element_type=F32) + b2_ref[0]
        bits = lax.bitcast_convert_type(y.astype(BF16).astype(F32), U32)
        return bits[:, D_MODEL // PACK:] | (bits[:, :D_MODEL // PACK] >> 16)

    @pl.when(e == 0)
    def _():
        chunk_start(0, 0)
        chunk_start(0, 1)
        for c in range(n_chunks):
            chunk_finish(0, c, 0)
            if c + 2 < n_chunks:
                chunk_start(0, c + 2)
        ybuf[...] = jnp.zeros(ybuf.shape, U32)
        for i in range(N_EXPERTS):
            block_copy(0, n_planes_rows + i * bm).start()
        for i in range(N_EXPERTS):
            block_copy(0, n_planes_rows + i * bm).wait()

    @pl.when(e + 1 < n_exp)
    def _():
        chunk_start(e + 1, 0)
        chunk_start(e + 1, 1)

    @pl.when(nb > 0)
    def _():
        @pl.when(b0 == 0)
        def _():
            gather(0, 0)
            y = mlp(0)
            gather(1, 1)
            ybuf[0] = y
            block_copy(1, spare_block).start()
            stream_next(0)

        def block_body(blk, slot):
            y = mlp(slot)
            gather(blk + 1, 1 - slot)
            scatter_start(blk - 1, 1 - slot)
            scatter_wait(slot)
            ybuf[slot] = y

        def block(i, carry):
            blk = b0 + i
            for slot in range(2):
                @pl.when(blk % 2 == slot)
                def _():
                    block_body(blk, slot)
            stream_next(i)
            return carry

        lax.fori_loop(jnp.where(b0 == 0, 1, 0), nb, block, 0)

    lax.fori_loop(jnp.minimum(nb, n_chunks), n_chunks, lambda c, carry: (stream_next(c), carry)[1], 0)

    @pl.when(e == pl.num_programs(0) - 1)
    def _():
        for last in range(2):
            @pl.when((total - 1) % 2 == last)
            def _():
                scatter_start(total - 1, last)
        scatter_wait(0)
        scatter_wait(1)


def _experts(u2p, blk_cnt, blk_start, blk_total, slot_tok, slot_row, n_tok, w1, b1, w2, b2):
    d = D_MODEL
    emap = lambda e, *_: (e, 0, 0)
    n_rows = TOP_K * n_tok + (N_EXPERTS + 1) * MOE_BM
    grid_spec = pltpu.PrefetchScalarGridSpec(
        num_scalar_prefetch=5,
        grid=(N_EXPERTS,),
        in_specs=[pl.BlockSpec(u2p.shape, lambda e, *_: (0, 0), pipeline_mode=pl.Buffered(1)),
                  pl.BlockSpec(memory_space=pl.ANY),
                  pl.BlockSpec((1, 1, 2 * D_FF), emap),
                  pl.BlockSpec(memory_space=pl.ANY),
                  pl.BlockSpec((1, 1, d), emap)],
        out_specs=pl.BlockSpec(memory_space=pl.ANY),
        scratch_shapes=[pltpu.VMEM((2, MOE_BM * PACK_ROWS, LANES), U32),
                        pltpu.VMEM((2, MOE_BM, d // PACK), U32),
                        pltpu.VMEM((2, d, 2 * D_FF), BF16),
                        pltpu.VMEM((2, D_FF, d), BF16),
                        pltpu.VMEM((2, W_CHUNK, 2 * D_FF), F32),
                        pltpu.VMEM((2, W_CHUNK, d), F32),
                        pltpu.SemaphoreType.DMA((2,)),
                        pltpu.SemaphoreType.DMA((2,)),
                        pltpu.SemaphoreType.DMA((2,))],
    )
    return pl.pallas_call(
        _expert_kernel,
        grid_spec=grid_spec,
        out_shape=jax.ShapeDtypeStruct((n_rows, d // PACK), U32),
        compiler_params=pltpu.CompilerParams(vmem_limit_bytes=VMEM_LIMIT,
                                             dimension_semantics=("arbitrary",)),
        name="experts",
    )(blk_cnt, blk_start, blk_total, slot_tok, slot_row, u2p,
      w1.astype(F32), b1.reshape(N_EXPERTS, 1, 2 * D_FF), w2.astype(F32),
      b2.reshape(N_EXPERTS, 1, d))


def _combine_kernel(h1_ref, gate_ref, g_ref, y0_ref, y1_ref, y2_ref, y3_ref, out_ref):
    acc = h1_ref[...]
    gates = gate_ref[...]
    for kk, y_ref in enumerate((y0_ref, y1_ref, y2_ref, y3_ref)):
        word = y_ref[...]
        lo = lax.bitcast_convert_type(word << 16, F32)
        hi = lax.bitcast_convert_type(word & jnp.uint32(0xFFFF0000), F32)
        acc = acc + gates[:, kk:kk + 1] * jnp.concatenate([lo, hi], axis=1)
    out_ref[...] = _rms_norm(acc, g_ref[...])


def _combine(h1, gates, final_g, ys):
    n, d = h1.shape
    tt = COMB_TILE
    nt = n // tt
    plane = lambda kk: pl.BlockSpec((tt, d // PACK), lambda i, kk=kk: (kk * nt + i, 0))
    return pl.pallas_call(
        _combine_kernel,
        grid=(nt,),
        in_specs=[pl.BlockSpec((tt, d), lambda i: (i, 0)),
                  pl.BlockSpec((tt, LANES), lambda i: (i, 0)),
                  pl.BlockSpec((1, d), lambda i: (0, 0))] + [plane(kk) for kk in range(TOP_K)],
        out_specs=pl.BlockSpec((tt, d), lambda i: (i, 0)),
        out_shape=jax.ShapeDtypeStruct((n, d), F32),
        compiler_params=pltpu.CompilerParams(vmem_limit_bytes=VMEM_LIMIT,
                                             dimension_semantics=("arbitrary",)),
        name="combine",
    )(h1, gates, final_g.reshape(1, d).astype(F32), ys, ys, ys, ys)


def _routing_tables(top_idx):
    n_tok = top_idx.shape[1]
    n_asg = n_tok * TOP_K
    bm = MOE_BM
    flat_e = top_idx.reshape(-1).astype(jnp.int32)
    sorted_e, order = lax.sort((flat_e, jnp.arange(n_asg, dtype=jnp.int32)), num_keys=1)
    experts = jnp.arange(N_EXPERTS, dtype=jnp.int32)
    counts = jnp.sum((flat_e[:, None] == experts[None, :]).astype(jnp.int32), axis=0)
    grp_start = jnp.cumsum(counts).astype(jnp.int32) - counts
    blk_cnt = (counts + bm - 1) // bm
    blk_end = jnp.cumsum(blk_cnt).astype(jnp.int32)
    blk_start = blk_end - blk_cnt
    n_blocks = n_asg // bm + N_EXPERTS
    blk = jnp.arange(n_blocks, dtype=jnp.int32)
    blk_e = jnp.minimum(jnp.sum((blk[:, None] >= blk_end[None, :]).astype(jnp.int32), axis=1),
                        N_EXPERTS - 1)
    e_count = counts[blk_e][:, None]
    e_first = grp_start[blk_e][:, None]
    within = ((blk - blk_start[blk_e]) * bm)[:, None] + jnp.arange(bm, dtype=jnp.int32)[None, :]
    real = within < e_count
    asg = order[jnp.where(real, e_first + within, 0)]
    tok = asg % n_tok
    pad_ord = blk[:, None] * bm + jnp.arange(bm, dtype=jnp.int32)[None, :] - (
        e_first + jnp.minimum(within, e_count))
    slot_tok = jnp.where(real, tok * PACK_ROWS, 0).astype(jnp.int32).reshape(-1)
    slot_row = jnp.where(real, asg,
                         n_asg + pad_ord).astype(jnp.int32).reshape(-1)
    return blk_cnt, blk_start, blk_end[-1:], slot_tok, slot_row


def kernel(x, meta_tokens, norm_mix_g, w_in, b_in, conv_dw_w, conv_dw_b, conv_ln_g, conv_ln_b, conv_pw_w, conv_pw_b, mlstm_norm_g, mlstm_out_w, w_out, norm_ffn_g, router_w, router_b, expert_w1, expert_b1, expert_w2, expert_b2, final_norm_g):
    bsz, seq, d = x.shape
    n_tok = bsz * seq
    assert d == D_MODEL and seq % SEQ_TILE == 0 and n_tok % COMB_TILE == 0 and w_in.shape[0] == 1
    weights = _mixer_weights(norm_mix_g, w_in, b_in, conv_dw_w, conv_dw_b, conv_ln_g, conv_ln_b,
                             conv_pw_w, conv_pw_b, mlstm_norm_g, mlstm_out_w, w_out, norm_ffn_g,
                             router_w, router_b)
    h1, u2p, idx, gates = _mixer(x.astype(F32), meta_tokens, weights)
    h1 = h1.reshape(n_tok, d)
    gates = gates.reshape(n_tok, LANES)
    top_idx = idx[:TOP_K]
    blk_cnt, blk_start, blk_total, slot_tok, slot_row = _routing_tables(top_idx)
    ys = _experts(u2p, blk_cnt, blk_start, blk_total, slot_tok, slot_row, n_tok,
                  expert_w1[0], expert_b1[0], expert_w2[0], expert_b2[0])
    out = _combine(h1, gates, final_norm_g, ys)
    return out.reshape(bsz, seq, d)
```

```python
import jax
import jax.numpy as jnp
from jax import lax
from jax.experimental import pallas as pl
from jax.experimental.pallas import tpu as pltpu

F32 = jnp.float32
BF16 = jnp.bfloat16
U32 = jnp.uint32

D_MODEL = 1024
N_META = 16
CONV_WIDTH = 31
M_HEADS = 4
M_DK = 128
M_DV = 256
QK_SCALE = M_DK ** -0.5
N_EXPERTS = 32
TOP_K = 4
D_FF = D_MODEL
SWIGLU_ALPHA = 1.702
SWIGLU_LIMIT = 7.0
NORM_EPS = 1e-5
NEG_GATE = -1.0e4

LANES = 128
SUBLANES = 8
VMEM_LIMIT = 56 * 1024 * 1024

SEQ_TILE = 256
CONV_HIST = 32
CONV_ROWS = 64
PROJ_CHUNK = 256
MOE_BM = 128
W_CHUNK = 128
COMB_TILE = 1024
PACK = 2
PACK_ROWS = D_MODEL // (PACK * LANES)

C_GLU = 0
C_Q = C_GLU + 2 * D_MODEL
C_K = C_Q + M_HEADS * M_DK
C_V = C_K + M_HEADS * M_DK
C_IF = C_V + M_HEADS * M_DV
C_O = C_IF + LANES
C_GC = C_O + D_MODEL
C_GM = C_GC + D_MODEL
N_PROJ = C_GM + D_MODEL


def _rms_norm(x, g):
    return x * lax.rsqrt(jnp.mean(x * x, axis=-1, keepdims=True) + NORM_EPS) * g


def _sigmoid(x):
    return 0.5 * jnp.tanh(0.5 * x) + 0.5


def _split_bf16(x, terms):
    parts = []
    for _ in range(terms - 1):
        p = x.astype(BF16)
        parts.append(p)
        x = x - p.astype(F32)
    parts.append(x.astype(BF16))
    return parts


def _proj_nobias(u_bf, w, lo, hi):
    assert hi <= C_O or lo >= C_O
    w_cols = w["win"][:, lo:hi] if hi <= C_O else w["win2"][:, lo - C_O:hi - C_O]
    return jnp.dot(u_bf, w_cols, preferred_element_type=F32)


def _proj(u_bf, w, lo, hi):
    return _proj_nobias(u_bf, w, lo, hi) + w["bin"][:, lo:hi]


def _glu(u_bf, w, row_valid):
    zg = _proj(u_bf, w, C_GLU, C_Q)
    y = zg[:, :D_MODEL] * _sigmoid(zg[:, D_MODEL:])
    if row_valid is not None:
        y = jnp.where(row_valid, y, 0.0)
    return y


def _conv_block(ybuf, dww_ref, dwb_ref, c, j):
    first = CONV_HIST - (CONV_WIDTH - 1)
    cs = slice(j * LANES, (j + 1) * LANES)
    win = ybuf[c * CONV_ROWS:c * CONV_ROWS + CONV_ROWS + CONV_HIST, cs]
    acc = jnp.broadcast_to(dwb_ref[:, cs], (CONV_ROWS, LANES))
    for phase in range(SUBLANES):
        taps = [k for k in range(CONV_WIDTH) if (-(first + k)) % SUBLANES == phase]
        if not taps:
            continue
        wb = win if phase == 0 else pltpu.roll(win, phase, axis=0)
        for k in taps:
            i0 = first + k + phase
            acc = acc + dww_ref[k:k + 1, cs] * wb[i0:i0 + CONV_ROWS, :]
    return acc


def _mlstm(zcol, w, c_st, n_st, m_st, row_valid, want_h):
    t = SEQ_TILE
    zif = zcol(C_IF, C_O)
    ig_all = zif
    lf_all = jax.nn.log_sigmoid(zif)
    if row_valid is not None:
        ig_all = jnp.where(row_valid, ig_all, NEG_GATE)
        lf_all = jnp.where(row_valid, lf_all, 0.0)
    rows = lax.broadcasted_iota(jnp.int32, (t, t), 0)
    cols = lax.broadcasted_iota(jnp.int32, (t, t), 1)
    causal = cols <= rows
    lf_terms = jnp.concatenate(_split_bf16(lf_all, 3), axis=1)
    b3 = jnp.dot(causal.astype(BF16), lf_terms, preferred_element_type=F32)
    b_all = b3[:, 0:LANES] + b3[:, LANES:2 * LANES] + b3[:, 2 * LANES:3 * LANES]
    b_sh = pltpu.roll(b_all, LANES - M_HEADS, axis=1)
    a_all = ig_all - b_sh
    a_rows = a_all.T
    hs = []
    for h in range(M_HEADS):
        a_row = a_rows[h:h + 1, :]
        a_col = a_all[:, h:h + 1]
        b_col = b_sh[:, h:h + 1]
        m_prev = m_st[h][0:1, 0:1]
        n_prev = n_st[h][0:1, :]
        ct = c_st[h]
        k = zcol(C_K + h * M_DK, C_K + (h + 1) * M_DK)
        v_bf = zcol(C_V + h * M_DV, C_V + (h + 1) * M_DV).astype(BF16)
        amat = jnp.where(causal, a_row, -jnp.inf)
        m_run = jnp.maximum(jnp.max(amat, axis=-1, keepdims=True), m_prev)
        if want_h:
            q = zcol(C_Q + h * M_DK, C_Q + (h + 1) * M_DK) * QK_SCALE
            q_bf = q.astype(BF16)
            s = lax.dot_general(q_bf, k.astype(BF16), (((1,), (1,)), ((), ())),
                                preferred_element_type=F32)
            p = s * jnp.exp(amat - m_run)
            inter_sc = jnp.exp(m_prev - m_run)
            num = (jnp.dot(p.astype(BF16), v_bf, preferred_element_type=F32)
                   + inter_sc * jnp.dot(q_bf, ct.astype(BF16), preferred_element_type=F32))
            den = (jnp.sum(p, axis=-1, keepdims=True)
                   + inter_sc * jnp.sum(q * n_prev, axis=-1, keepdims=True))
            scale = 1.0 / jnp.maximum(jnp.abs(den), jnp.exp(-(b_col + m_run)))
            hh = num * scale
            hh = hh * lax.rsqrt(jnp.mean(hh * hh, axis=-1, keepdims=True) + NORM_EPS)
            hs.append(hh * w["mng"][:, h * M_DV:(h + 1) * M_DV])
        m_last = m_run[t - 1:t, :]
        decay = jnp.exp(m_prev - m_last)
        kw = k * jnp.exp(a_col - m_last)
        c_st[h] = decay * ct + jnp.dot(kw.T.astype(BF16), v_bf, preferred_element_type=F32)
        n_new = decay * n_prev + jnp.sum(kw, axis=0, keepdims=True)
        n_st[h] = jnp.broadcast_to(n_new, (SUBLANES, M_DK))
        m_st[h] = jnp.broadcast_to(b_col[t - 1:t, :] + m_last, (SUBLANES, LANES))
    return jnp.concatenate(hs, axis=1) if want_h else None


def _router(u2, rw_ref, rb_ref):
    t = u2.shape[0]
    u_hi, u_mid = _split_bf16(u2, 2)
    lhs = jnp.concatenate([u_hi, u_mid, u_hi], axis=1)
    logits_t = lax.dot_general(rw_ref[...], lhs, (((1,), (1,)), ((), ())),
                               preferred_element_type=F32)
    cur = logits_t[0:N_EXPERTS, :] + rb_ref[:, 0:1]
    row = lax.broadcasted_iota(jnp.int32, (N_EXPERTS, t), 0)
    idx_rows = []
    exp_rows = []
    top0 = None
    esum = None
    for kk in range(TOP_K):
        mx = jnp.max(cur, axis=0, keepdims=True)
        ix = jnp.min(jnp.where(cur == mx, row, N_EXPERTS), axis=0, keepdims=True)
        if kk == 0:
            top0 = mx
        ek = jnp.exp(mx - top0)
        esum = ek if kk == 0 else esum + ek
        idx_rows.append(ix)
        exp_rows.append(ek)
        cur = jnp.where(row == ix, -jnp.inf, cur)
    idx_t = jnp.concatenate(idx_rows + [jnp.zeros((SUBLANES - TOP_K, t), jnp.int32)], axis=0)
    gates_t = jnp.concatenate([r / esum for r in exp_rows]
                              + [jnp.zeros((LANES - TOP_K, t), F32)], axis=0)
    return idx_t, gates_t.T


def _pack_rows(u2, out_ref):
    t = u2.shape[0]
    bits = lax.bitcast_convert_type(u2.astype(BF16).astype(F32), U32)
    for i in range(PACK_ROWS):
        lo = bits[:, i * 2 * LANES:i * 2 * LANES + LANES]
        hi = bits[:, i * 2 * LANES + LANES:(i + 1) * 2 * LANES]
        out_ref[pl.ds(i, t, stride=PACK_ROWS), :] = hi | (lo >> 16)


def _unpack_rows(tile_ref, slot, rows):
    parts = []
    for i in range(PACK_ROWS):
        word = tile_ref.at[slot][pl.ds(i, rows, stride=PACK_ROWS), :]
        lo = lax.bitcast_convert_type(word << 16, F32)
        hi = lax.bitcast_convert_type(word & jnp.uint32(0xFFFF0000), F32)
        parts += [lo.astype(BF16), hi.astype(BF16)]
    return jnp.concatenate(parts, axis=1)


def _weights(refs):
    names = ("g1", "win", "win2", "bin", "dww", "dwb", "lng", "lnb", "pww", "pwb", "mng", "mow",
             "wout", "g2", "rw", "rb")
    return dict(zip(names, refs))


N_WEIGHTS = 16


def _prefix_kernel(*refs):
    x_ref = refs[0]
    w = _weights(refs[1:1 + N_WEIGHTS])
    ytail_ref, c_out, n_out, m_out = refs[1 + N_WEIGHTS:]
    t = SEQ_TILE
    row_valid = lax.broadcasted_iota(jnp.int32, (t, 1), 0) >= t - N_META
    c_out[...] = jnp.zeros(c_out.shape, F32)
    n_out[...] = jnp.zeros(n_out.shape, F32)
    m_out[...] = jnp.zeros(m_out.shape, F32)
    u_bf = _rms_norm(x_ref[...], w["g1"][...]).astype(BF16)
    y = _glu(u_bf, w, row_valid)
    ytail_ref[...] = y[t - CONV_HIST:, :]
    _mlstm(lambda lo, hi: _proj(u_bf, w, lo, hi), w, c_out, n_out, m_out, row_valid, want_h=False)


def _mixer_kernel(*refs):
    x_ref = refs[0]
    w = _weights(refs[1:1 + N_WEIGHTS])
    y0_ref, c0_ref, n0_ref, m0_ref = refs[1 + N_WEIGHTS:5 + N_WEIGHTS]
    h1_ref, u2p_ref, idx_ref, gate_ref = refs[5 + N_WEIGHTS:9 + N_WEIGHTS]
    ybuf, cbuf, zbuf, c_st, n_st, m_st = refs[9 + N_WEIGHTS:]
    t = SEQ_TILE

    @pl.when(pl.program_id(1) == 0)
    def _():
        ybuf[0:CONV_HIST, :] = y0_ref[...]
        c_st[...] = c0_ref[...]
        n_st[...] = n0_ref[...]
        m_st[...] = m0_ref[...]

    x = x_ref[0]
    u_bf = _rms_norm(x, w["g1"][...]).astype(BF16)

    ybuf[CONV_HIST:CONV_HIST + t, :] = _glu(u_bf, w, None)
    chunks = [(lo, min(lo + PROJ_CHUNK, end)) for start, end in ((C_Q, C_O), (C_O, N_PROJ))
              for lo in range(start, end, PROJ_CHUNK)]
    blocks = [(c, j) for c in range(t // CONV_ROWS) for j in range(D_MODEL // LANES)]
    assert len(chunks) <= len(blocks)
    for n, (c, j) in enumerate(blocks):
        cbuf[c * CONV_ROWS:(c + 1) * CONV_ROWS, j * LANES:(j + 1) * LANES] = _conv_block(
            ybuf, w["dww"], w["dwb"], c, j)
        if n < len(chunks):
            lo, hi = chunks[n]
            zbuf[:, lo - C_Q:hi - C_Q] = _proj_nobias(u_bf, w, lo, hi)
    ybuf[0:CONV_HIST, :] = ybuf[t:t + CONV_HIST, :]
    zcol = lambda lo, hi: zbuf[:, lo - C_Q:hi - C_Q] + w["bin"][:, lo:hi]

    conv = cbuf[...]
    mu = jnp.mean(conv, axis=-1, keepdims=True)
    cen = conv - mu
    var = jnp.mean(cen * cen, axis=-1, keepdims=True)
    ln = cen * lax.rsqrt(var + NORM_EPS) * w["lng"][...] + w["lnb"][...]
    conv_out = jnp.dot((ln * _sigmoid(ln)).astype(BF16), w["pww"][...],
                       preferred_element_type=F32) + w["pwb"][...]

    hcat = _mlstm(zcol, w, c_st, n_st, m_st, None, want_h=True)
    o_gate = _sigmoid(zcol(C_O, C_GC))
    mlstm_out = jnp.dot((o_gate * hcat).astype(BF16), w["mow"][...], preferred_element_type=F32)

    g_conv = _sigmoid(zcol(C_GC, C_GM))
    g_mlstm = _sigmoid(zcol(C_GM, N_PROJ))
    mix = (g_conv * conv_out + g_mlstm * mlstm_out).astype(BF16)
    h1 = x + jnp.dot(mix, w["wout"][...], preferred_element_type=F32)
    h1_ref[0] = h1

    u2 = _rms_norm(h1, w["g2"][...])
    _pack_rows(u2, u2p_ref)
    idx, gates = _router(u2, w["rw"], w["rb"])
    idx_ref[...] = idx
    gate_ref[0] = gates


def _const_spec(shape):
    nd = len(shape)
    return pl.BlockSpec(shape, lambda *_: (0,) * nd, pipeline_mode=pl.Buffered(1))


def _mixer_weights(norm_mix_g, w_in, b_in, conv_dw_w, conv_dw_b, conv_ln_g, conv_ln_b, conv_pw_w,
                   conv_pw_b, mlstm_norm_g, mlstm_out_w, w_out, norm_ffn_g, router_w, router_b):
    d = D_MODEL
    oo = 3 * d + 2 * M_HEADS * M_DK + 2 * M_HEADS
    gate_pad = LANES - 2 * M_HEADS
    w_in0 = w_in[0]
    win_a = jnp.pad(w_in0[:, :oo].astype(BF16), ((0, 0), (0, gate_pad)))
    win_b = w_in0[:, oo:].astype(BF16)

    def regroup(a):
        return jnp.concatenate([a[:oo], jnp.zeros((gate_pad,), a.dtype), a[oo:]])

    row = lambda a: a.reshape(1, -1).astype(F32)
    dww = jnp.pad(conv_dw_w[0].astype(F32), ((0, 32 - CONV_WIDTH), (0, 0)))
    rw = jnp.pad(router_w[0].astype(F32), ((0, 0), (0, LANES - N_EXPERTS)))
    rw_hi = rw.astype(BF16)
    rw_mid = (rw - rw_hi.astype(F32)).astype(BF16)
    rb = jnp.broadcast_to(router_b[0].astype(F32)[:, None], (N_EXPERTS, LANES))
    return [row(norm_mix_g[0]), win_a, win_b, row(regroup(b_in[0])),
            dww, row(conv_dw_b[0]), row(conv_ln_g[0]), row(conv_ln_b[0]),
            conv_pw_w[0].astype(BF16), row(conv_pw_b[0]), row(mlstm_norm_g[0]),
            mlstm_out_w[0].astype(BF16), w_out[0].astype(BF16), row(norm_ffn_g[0]),
            jnp.concatenate([rw_hi, rw_hi, rw_mid], axis=0).T, rb]


def _mixer(x, meta_tokens, weights):
    bsz, seq, d = x.shape
    t = SEQ_TILE
    n_seq = seq // t
    w_specs = [_const_spec(a.shape) for a in weights]
    state_shapes = [(CONV_HIST, d), (M_HEADS, M_DK, M_DV), (M_HEADS, SUBLANES, M_DK),
                    (M_HEADS, SUBLANES, LANES)]

    x_meta = jnp.concatenate([jnp.zeros((t - N_META, d), F32), meta_tokens.astype(F32)], axis=0)
    state = pl.pallas_call(
        _prefix_kernel,
        grid=(1,),
        in_specs=[pl.BlockSpec((t, d), lambda i: (0, 0))] + w_specs,
        out_specs=[pl.BlockSpec(s, lambda i, n=len(s): (0,) * n) for s in state_shapes],
        out_shape=[jax.ShapeDtypeStruct(s, F32) for s in state_shapes],
        compiler_params=pltpu.CompilerParams(vmem_limit_bytes=VMEM_LIMIT),
        name="prefix",
    )(x_meta, *weights)

    tile = lambda b, s: (b, s, 0)
    return pl.pallas_call(
        _mixer_kernel,
        grid=(bsz, n_seq),
        in_specs=([pl.BlockSpec((1, t, d), tile)] + w_specs
                  + [_const_spec(s) for s in state_shapes]),
        out_specs=[pl.BlockSpec((1, t, d), tile),
                   pl.BlockSpec((t * PACK_ROWS, LANES), lambda b, s: (b * n_seq + s, 0)),
                   pl.BlockSpec((SUBLANES, t), lambda b, s: (0, b * n_seq + s)),
                   pl.BlockSpec((1, t, LANES), tile)],
        out_shape=[jax.ShapeDtypeStruct((bsz, seq, d), F32),
                   jax.ShapeDtypeStruct((bsz * seq * PACK_ROWS, LANES), U32),
                   jax.ShapeDtypeStruct((SUBLANES, bsz * seq), jnp.int32),
                   jax.ShapeDtypeStruct((bsz, seq, LANES), F32)],
        scratch_shapes=[pltpu.VMEM((CONV_HIST + t, d), F32),
                        pltpu.VMEM((t, d), F32),
                        pltpu.VMEM((t, N_PROJ - C_Q), F32),
                        pltpu.VMEM((M_HEADS, M_DK, M_DV), F32),
                        pltpu.VMEM((M_HEADS, SUBLANES, M_DK), F32),
                        pltpu.VMEM((M_HEADS, SUBLANES, LANES), F32)],
        compiler_params=pltpu.CompilerParams(vmem_limit_bytes=VMEM_LIMIT,
                                             dimension_semantics=("arbitrary", "arbitrary")),
        name="mixer",
    )(x, *weights, *state)


def _chunk_rows(c):
    start = c * W_CHUNK
    return pl.ds(start if isinstance(c, int) else pl.multiple_of(start, W_CHUNK), W_CHUNK)


def _expert_kernel(cnt_ref, start_ref, total_ref, tok_ref, row_ref,
                   x_ref, w1_hbm, b1_ref, w2_hbm, b2_ref, ys_hbm,
                   tile, ybuf, wb1, wb2, st1, st2, osem, wsem1, wsem2):
    e = pl.program_id(0)
    n_exp = pl.num_programs(0)
    nb = cnt_ref[e]
    b0 = start_ref[e]
    total = total_ref[0]
    bm = MOE_BM
    n_planes_rows = ys_hbm.shape[0] - (N_EXPERTS + 1) * bm
    spare_block = n_planes_rows + N_EXPERTS * bm
    wcur = e % 2
    n_chunks = D_MODEL // W_CHUNK

    def chunk_copies(ex, c):
        rows = _chunk_rows(c)
        s = c % 2
        return (pltpu.make_async_copy(w1_hbm.at[ex, rows, :], st1.at[s], wsem1.at[s]),
                pltpu.make_async_copy(w2_hbm.at[ex, rows, :], st2.at[s], wsem2.at[s]))

    def chunk_start(ex, c):
        for cp in chunk_copies(ex, c):
            cp.start()

    def chunk_finish(ex, c, wslot):
        for cp in chunk_copies(ex, c):
            cp.wait()
        rows = _chunk_rows(c)
        wb1[wslot, rows, :] = st1[c % 2].astype(BF16)
        wb2[wslot, rows, :] = st2[c % 2].astype(BF16)

    def stream_next(c):
        c = jnp.asarray(c, jnp.int32)

        @pl.when(jnp.logical_and(c < n_chunks, e + 1 < n_exp))
        def _():
            chunk_finish(e + 1, c, 1 - wcur)

            @pl.when(c + 2 < n_chunks)
            def _():
                chunk_start(e + 1, c + 2)

    def gather(blk, slot):
        for r in range(bm):
            src = pl.multiple_of(tok_ref[blk * bm + r], PACK_ROWS)
            tile[slot, pl.ds(PACK_ROWS * r, PACK_ROWS), :] = x_ref[pl.ds(src, PACK_ROWS), :]

    def scatter_start(blk, slot):
        for r in range(bm):
            dst = row_ref[blk * bm + r]
            pltpu.make_async_copy(ybuf.at[slot, pl.ds(r, 1), :], ys_hbm.at[pl.ds(dst, 1), :],
                                  osem.at[slot]).start(priority=r % 2)

    def block_copy(slot, row0):
        return pltpu.make_async_copy(ybuf.at[slot], ys_hbm.at[pl.ds(row0, bm), :], osem.at[slot])

    def scatter_wait(slot):
        block_copy(slot, 0).wait()

    def mlp(slot):
        xb = _unpack_rows(tile, slot, bm)
        hcat = jnp.dot(xb, wb1[wcur], preferred_element_type=F32) + b1_ref[0]
        h_glu = jnp.minimum(hcat[:, :D_FF], SWIGLU_LIMIT)
        h_lin = jnp.clip(hcat[:, D_FF:], -SWIGLU_LIMIT, SWIGLU_LIMIT)
        act = h_glu * _sigmoid(SWIGLU_ALPHA * h_glu) * (h_lin + 1.0)
        y = jnp.dot(act.astype(BF16), wb2[wcur], preferred_element_type=F32) + b2_ref[0]
        bits = lax.bitcast_convert_type(y.astype(BF16).astype(F32), U32)
        return bits[:, D_MODEL // PACK:] | (bits[:, :D_MODEL // PACK] >> 16)

    @pl.when(e == 0)
    def _():
        chunk_start(0, 0)
        chunk_start(0, 1)
        for c in range(n_chunks):
            chunk_finish(0, c, 0)
            if c + 2 < n_chunks:
                chunk_start(0, c + 2)
        ybuf[...] = jnp.zeros(ybuf.shape, U32)
        for i in range(N_EXPERTS):
            block_copy(0, n_planes_rows + i * bm).start()
        for i in range(N_EXPERTS):
            block_copy(0, n_planes_rows + i * bm).wait()

    @pl.when(e + 1 < n_exp)
    def _():
        chunk_start(e + 1, 0)
        chunk_start(e + 1, 1)

    @pl.when(nb > 0)
    def _():
        @pl.when(b0 == 0)
        def _():
            gather(0, 0)
            y = mlp(0)
            gather(1, 1)
            ybuf[0] = y
            block_copy(1, spare_block).start()
            stream_next(0)

        def block_body(blk, slot):
            y = mlp(slot)
            gather(blk + 1, 1 - slot)
            scatter_start(blk - 1, 1 - slot)
            scatter_wait(slot)
            ybuf[slot] = y

        def block(i, carry):
            blk = b0 + i
            for slot in range(2):
                @pl.when(blk % 2 == slot)
                def _():
                    block_body(blk, slot)
            stream_next(i)
            return carry

        lax.fori_loop(jnp.where(b0 == 0, 1, 0), nb, block, 0)

    lax.fori_loop(jnp.minimum(nb, n_chunks), n_chunks, lambda c, carry: (stream_next(c), carry)[1], 0)

    @pl.when(e == pl.num_programs(0) - 1)
    def _():
        for last in range(2):
            @pl.when((total - 1) % 2 == last)
            def _():
                scatter_start(total - 1, last)
        scatter_wait(0)
        scatter_wait(1)


def _experts(u2p, blk_cnt, blk_start, blk_total, slot_tok, slot_row, n_tok, w1, b1, w2, b2):
    d = D_MODEL
    emap = lambda e, *_: (e, 0, 0)
    n_rows = TOP_K * n_tok + (N_EXPERTS + 1) * MOE_BM
    grid_spec = pltpu.PrefetchScalarGridSpec(
        num_scalar_prefetch=5,
        grid=(N_EXPERTS,),
        in_specs=[pl.BlockSpec(u2p.shape, lambda e, *_: (0, 0), pipeline_mode=pl.Buffered(1)),
                  pl.BlockSpec(memory_space=pl.ANY),
                  pl.BlockSpec((1, 1, 2 * D_FF), emap),
                  pl.BlockSpec(memory_space=pl.ANY),
                  pl.BlockSpec((1, 1, d), emap)],
        out_specs=pl.BlockSpec(memory_space=pl.ANY),
        scratch_shapes=[pltpu.VMEM((2, MOE_BM * PACK_ROWS, LANES), U32),
                        pltpu.VMEM((2, MOE_BM, d // PACK), U32),
                        pltpu.VMEM((2, d, 2 * D_FF), BF16),
                        pltpu.VMEM((2, D_FF, d), BF16),
                        pltpu.VMEM((2, W_CHUNK, 2 * D_FF), F32),
                        pltpu.VMEM((2, W_CHUNK, d), F32),
                        pltpu.SemaphoreType.DMA((2,)),
                        pltpu.SemaphoreType.DMA((2,)),
                        pltpu.SemaphoreType.DMA((2,))],
    )
    return pl.pallas_call(
        _expert_kernel,
        grid_spec=grid_spec,
        out_shape=jax.ShapeDtypeStruct((n_rows, d // PACK), U32),
        compiler_params=pltpu.CompilerParams(vmem_limit_bytes=VMEM_LIMIT,
                                             dimension_semantics=("arbitrary",)),
        name="experts",
    )(blk_cnt, blk_start, blk_total, slot_tok, slot_row, u2p,
      w1.astype(F32), b1.reshape(N_EXPERTS, 1, 2 * D_FF), w2.astype(F32),
      b2.reshape(N_EXPERTS, 1, d))


def _combine_kernel(h1_ref, gate_ref, g_ref, y0_ref, y1_ref, y2_ref, y3_ref, out_ref):
    acc = h1_ref[...]
    gates = gate_ref[...]
    for kk, y_ref in enumerate((y0_ref, y1_ref, y2_ref, y3_ref)):
        word = y_ref[...]
        lo = lax.bitcast_convert_type(word << 16, F32)
        hi = lax.bitcast_convert_type(word & jnp.uint32(0xFFFF0000), F32)
        acc = acc + gates[:, kk:kk + 1] * jnp.concatenate([lo, hi], axis=1)
    out_ref[...] = _rms_norm(acc, g_ref[...])


def _combine(h1, gates, final_g, ys):
    n, d = h1.shape
    tt = COMB_TILE
    nt = n // tt
    plane = lambda kk: pl.BlockSpec((tt, d // PACK), lambda i, kk=kk: (kk * nt + i, 0))
    return pl.pallas_call(
        _combine_kernel,
        grid=(nt,),
        in_specs=[pl.BlockSpec((tt, d), lambda i: (i, 0)),
                  pl.BlockSpec((tt, LANES), lambda i: (i, 0)),
                  pl.BlockSpec((1, d), lambda i: (0, 0))] + [plane(kk) for kk in range(TOP_K)],
        out_specs=pl.BlockSpec((tt, d), lambda i: (i, 0)),
        out_shape=jax.ShapeDtypeStruct((n, d), F32),
        compiler_params=pltpu.CompilerParams(vmem_limit_bytes=VMEM_LIMIT,
                                             dimension_semantics=("arbitrary",)),
        name="combine",
    )(h1, gates, final_g.reshape(1, d).astype(F32), ys, ys, ys, ys)


def _routing_tables(top_idx):
    n_tok = top_idx.shape[1]
    n_asg = n_tok * TOP_K
    bm = MOE_BM
    flat_e = top_idx.reshape(-1).astype(jnp.int32)
    sorted_e, order = lax.sort((flat_e, jnp.arange(n_asg, dtype=jnp.int32)), num_keys=1)
    experts = jnp.arange(N_EXPERTS, dtype=jnp.int32)
    counts = jnp.sum((flat_e[:, None] == experts[None, :]).astype(jnp.int32), axis=0)
    grp_start = jnp.cumsum(counts).astype(jnp.int32) - counts
    blk_cnt = (counts + bm - 1) // bm
    blk_end = jnp.cumsum(blk_cnt).astype(jnp.int32)
    blk_start = blk_end - blk_cnt
    n_blocks = n_asg // bm + N_EXPERTS
    blk = jnp.arange(n_blocks, dtype=jnp.int32)
    blk_e = jnp.minimum(jnp.sum((blk[:, None] >= blk_end[None, :]).astype(jnp.int32), axis=1),
                        N_EXPERTS - 1)
    e_count = counts[blk_e][:, None]
    e_first = grp_start[blk_e][:, None]
    within = ((blk - blk_start[blk_e]) * bm)[:, None] + jnp.arange(bm, dtype=jnp.int32)[None, :]
    real = within < e_count
    asg = order[jnp.where(real, e_first + within, 0)]
    tok = asg % n_tok
    pad_ord = blk[:, None] * bm + jnp.arange(bm, dtype=jnp.int32)[None, :] - (
        e_first + jnp.minimum(within, e_count))
    slot_tok = jnp.where(real, tok * PACK_ROWS, 0).astype(jnp.int32).reshape(-1)
    slot_row = jnp.where(real, asg,
                         n_asg + pad_ord).astype(jnp.int32).reshape(-1)
    return blk_cnt, blk_start, blk_end[-1:], slot_tok, slot_row


def kernel(x, meta_tokens, norm_mix_g, w_in, b_in, conv_dw_w, conv_dw_b, conv_ln_g, conv_ln_b, conv_pw_w, conv_pw_b, mlstm_norm_g, mlstm_out_w, w_out, norm_ffn_g, router_w, router_b, expert_w1, expert_b1, expert_w2, expert_b2, final_norm_g):
    bsz, seq, d = x.shape
    n_tok = bsz * seq
    assert d == D_MODEL and seq % SEQ_TILE == 0 and n_tok % COMB_TILE == 0 and w_in.shape[0] == 1
    weights = _mixer_weights(norm_mix_g, w_in, b_in, conv_dw_w, conv_dw_b, conv_ln_g, conv_ln_b,
                             conv_pw_w, conv_pw_b, mlstm_norm_g, mlstm_out_w, w_out, norm_ffn_g,
                             router_w, router_b)
    h1, u2p, idx, gates = _mixer(x.astype(F32), meta_tokens, weights)
    h1 = h1.reshape(n_tok, d)
    gates = gates.reshape(n_tok, LANES)
    top_idx = idx[:TOP_K]
    blk_cnt, blk_start, blk_total, slot_tok, slot_row = _routing_tables(top_idx)
    ys = _experts(u2p, blk_cnt, blk_start, blk_total, slot_tok, slot_row, n_tok,
                  expert_w1[0], expert_b1[0], expert_w2[0], expert_b2[0])
    out = _combine(h1, gates, final_norm_g, ys)
    return out.reshape(bsz, seq, d)
```

```python
import jax
import jax.numpy as jnp
from jax import lax
from jax.experimental import pallas as pl
from jax.experimental.pallas import tpu as pltpu

F32 = jnp.float32
BF16 = jnp.bfloat16
U32 = jnp.uint32

D_MODEL = 1024
N_META = 16
CONV_WIDTH = 31
M_HEADS = 4
M_DK = 128
M_DV = 256
QK_SCALE = M_DK ** -0.5
N_EXPERTS = 32
TOP_K = 4
D_FF = D_MODEL
SWIGLU_ALPHA = 1.702
SWIGLU_LIMIT = 7.0
NORM_EPS = 1e-5
NEG_GATE = -1.0e4

LANES = 128
SUBLANES = 8
VMEM_LIMIT = 56 * 1024 * 1024

SEQ_TILE = 256
CONV_HIST = 32
CONV_ROWS = 32
PROJ_CHUNK = 256
MOE_BM = 128
W_CHUNK = 128
COMB_TILE = 1024
PACK = 2
PACK_ROWS = D_MODEL // (PACK * LANES)

C_GLU = 0
C_Q = C_GLU + 2 * D_MODEL
C_K = C_Q + M_HEADS * M_DK
C_V = C_K + M_HEADS * M_DK
C_IF = C_V + M_HEADS * M_DV
C_O = C_IF + LANES
C_GC = C_O + D_MODEL
C_GM = C_GC + D_MODEL
N_PROJ = C_GM + D_MODEL


def _rms_norm(x, g):
    return x * lax.rsqrt(jnp.mean(x * x, axis=-1, keepdims=True) + NORM_EPS) * g


def _sigmoid(x):
    return 0.5 * jnp.tanh(0.5 * x) + 0.5


def _split_bf16(x, terms):
    parts = []
    for _ in range(terms - 1):
        p = x.astype(BF16)
        parts.append(p)
        x = x - p.astype(F32)
    parts.append(x.astype(BF16))
    return parts


def _proj_nobias(u_bf, w, lo, hi):
    assert hi <= C_O or lo >= C_O
    w_cols = w["win"][:, lo:hi] if hi <= C_O else w["win2"][:, lo - C_O:hi - C_O]
    return jnp.dot(u_bf, w_cols, preferred_element_type=F32)


def _proj(u_bf, w, lo, hi):
    return _proj_nobias(u_bf, w, lo, hi) + w["bin"][:, lo:hi]


def _glu(u_bf, w, row_valid):
    zg = _proj(u_bf, w, C_GLU, C_Q)
    y = zg[:, :D_MODEL] * _sigmoid(zg[:, D_MODEL:])
    if row_valid is not None:
        y = jnp.where(row_valid, y, 0.0)
    return y


def _conv_block(ybuf, dww_ref, dwb_ref, c, j):
    first = CONV_HIST - (CONV_WIDTH - 1)
    cs = slice(j * LANES, (j + 1) * LANES)
    win = ybuf[c * CONV_ROWS:c * CONV_ROWS + CONV_ROWS + CONV_HIST, cs]
    acc = jnp.broadcast_to(dwb_ref[:, cs], (CONV_ROWS, LANES))
    for phase in range(SUBLANES):
        taps = [k for k in range(CONV_WIDTH) if (-(first + k)) % SUBLANES == phase]
        if not taps:
            continue
        wb = win if phase == 0 else pltpu.roll(win, phase, axis=0)
        for k in taps:
            i0 = first + k + phase
            acc = acc + dww_ref[k:k + 1, cs] * wb[i0:i0 + CONV_ROWS, :]
    return acc


def _mlstm(zcol, w, c_st, n_st, m_st, row_valid, want_h):
    t = SEQ_TILE
    zif = zcol(C_IF, C_O)
    ig_all = zif
    lf_all = jax.nn.log_sigmoid(zif)
    if row_valid is not None:
        ig_all = jnp.where(row_valid, ig_all, NEG_GATE)
        lf_all = jnp.where(row_valid, lf_all, 0.0)
    rows = lax.broadcasted_iota(jnp.int32, (t, t), 0)
    cols = lax.broadcasted_iota(jnp.int32, (t, t), 1)
    causal = cols <= rows
    lf_terms = jnp.concatenate(_split_bf16(lf_all, 3), axis=1)
    b3 = jnp.dot(causal.astype(BF16), lf_terms, preferred_element_type=F32)
    b_all = b3[:, 0:LANES] + b3[:, LANES:2 * LANES] + b3[:, 2 * LANES:3 * LANES]
    b_sh = pltpu.roll(b_all, LANES - M_HEADS, axis=1)
    a_all = ig_all - b_sh
    a_rows = a_all.T
    hs = []
    for h in range(M_HEADS):
        a_row = a_rows[h:h + 1, :]
        a_col = a_all[:, h:h + 1]
        b_col = b_sh[:, h:h + 1]
        m_prev = m_st[h][0:1, 0:1]
        n_prev = n_st[h][0:1, :]
        ct = c_st[h]
        k = zcol(C_K + h * M_DK, C_K + (h + 1) * M_DK)
        v_bf = zcol(C_V + h * M_DV, C_V + (h + 1) * M_DV).astype(BF16)
        amat = jnp.where(causal, a_row, -jnp.inf)
        m_run = jnp.maximum(jnp.max(amat, axis=-1, keepdims=True), m_prev)
        if want_h:
            q = zcol(C_Q + h * M_DK, C_Q + (h + 1) * M_DK) * QK_SCALE
            q_bf = q.astype(BF16)
            s = lax.dot_general(q_bf, k.astype(BF16), (((1,), (1,)), ((), ())),
                                preferred_element_type=F32)
            p = s * jnp.exp(amat - m_run)
            inter_sc = jnp.exp(m_prev - m_run)
            num = (jnp.dot(p.astype(BF16), v_bf, preferred_element_type=F32)
                   + inter_sc * jnp.dot(q_bf, ct.astype(BF16), preferred_element_type=F32))
            den = (jnp.sum(p, axis=-1, keepdims=True)
                   + inter_sc * jnp.sum(q * n_prev, axis=-1, keepdims=True))
            scale = 1.0 / jnp.maximum(jnp.abs(den), jnp.exp(-(b_col + m_run)))
            hh = num * scale
            hh = hh * lax.rsqrt(jnp.mean(hh * hh, axis=-1, keepdims=True) + NORM_EPS)
            hs.append(hh * w["mng"][:, h * M_DV:(h + 1) * M_DV])
        m_last = m_run[t - 1:t, :]
        decay = jnp.exp(m_prev - m_last)
        kw = k * jnp.exp(a_col - m_last)
        c_st[h] = decay * ct + jnp.dot(kw.T.astype(BF16), v_bf, preferred_element_type=F32)
        n_new = decay * n_prev + jnp.sum(kw, axis=0, keepdims=True)
        n_st[h] = jnp.broadcast_to(n_new, (SUBLANES, M_DK))
        m_st[h] = jnp.broadcast_to(b_col[t - 1:t, :] + m_last, (SUBLANES, LANES))
    return jnp.concatenate(hs, axis=1) if want_h else None


def _router(u2, rw_ref, rb_ref):
    t = u2.shape[0]
    u_hi, u_mid = _split_bf16(u2, 2)
    lhs = jnp.concatenate([u_hi, u_mid, u_hi], axis=1)
    logits_t = lax.dot_general(rw_ref[...], lhs, (((1,), (1,)), ((), ())),
                               preferred_element_type=F32)
    cur = logits_t[0:N_EXPERTS, :] + rb_ref[:, 0:1]
    row = lax.broadcasted_iota(jnp.int32, (N_EXPERTS, t), 0)
    idx_rows = []
    exp_rows = []
    top0 = None
    esum = None
    for kk in range(TOP_K):
        mx = jnp.max(cur, axis=0, keepdims=True)
        ix = jnp.min(jnp.where(cur == mx, row, N_EXPERTS), axis=0, keepdims=True)
        if kk == 0:
            top0 = mx
        ek = jnp.exp(mx - top0)
        esum = ek if kk == 0 else esum + ek
        idx_rows.append(ix)
        exp_rows.append(ek)
        cur = jnp.where(row == ix, -jnp.inf, cur)
    idx_t = jnp.concatenate(idx_rows + [jnp.zeros((SUBLANES - TOP_K, t), jnp.int32)], axis=0)
    gates_t = jnp.concatenate([r / esum for r in exp_rows]
                              + [jnp.zeros((LANES - TOP_K, t), F32)], axis=0)
    return idx_t, gates_t.T


def _pack_rows(u2, out_ref):
    t = u2.shape[0]
    bits = lax.bitcast_convert_type(u2.astype(BF16).astype(F32), U32)
    for i in range(PACK_ROWS):
        lo = bits[:, i * 2 * LANES:i * 2 * LANES + LANES]
        hi = bits[:, i * 2 * LANES + LANES:(i + 1) * 2 * LANES]
        out_ref[pl.ds(i, t, stride=PACK_ROWS), :] = hi | (lo >> 16)


def _unpack_rows(tile_ref, slot, rows):
    parts = []
    for i in range(PACK_ROWS):
        word = tile_ref.at[slot][pl.ds(i, rows, stride=PACK_ROWS), :]
        lo = lax.bitcast_convert_type(word << 16, F32)
        hi = lax.bitcast_convert_type(word & jnp.uint32(0xFFFF0000), F32)
        parts += [lo.astype(BF16), hi.astype(BF16)]
    return jnp.concatenate(parts, axis=1)


def _weights(refs):
    names = ("g1", "win", "win2", "bin", "dww", "dwb", "lng", "lnb", "pww", "pwb", "mng", "mow",
             "wout", "g2", "rw", "rb")
    return dict(zip(names, refs))


N_WEIGHTS = 16


def _prefix_kernel(*refs):
    x_ref = refs[0]
    w = _weights(refs[1:1 + N_WEIGHTS])
    ytail_ref, c_out, n_out, m_out = refs[1 + N_WEIGHTS:]
    t = SEQ_TILE
    row_valid = lax.broadcasted_iota(jnp.int32, (t, 1), 0) >= t - N_META
    c_out[...] = jnp.zeros(c_out.shape, F32)
    n_out[...] = jnp.zeros(n_out.shape, F32)
    m_out[...] = jnp.zeros(m_out.shape, F32)
    u_bf = _rms_norm(x_ref[...], w["g1"][...]).astype(BF16)
    y = _glu(u_bf, w, row_valid)
    ytail_ref[...] = y[t - CONV_HIST:, :]
    _mlstm(lambda lo, hi: _proj(u_bf, w, lo, hi), w, c_out, n_out, m_out, row_valid, want_h=False)


def _mixer_kernel(*refs):
    x_ref = refs[0]
    w = _weights(refs[1:1 + N_WEIGHTS])
    y0_ref, c0_ref, n0_ref, m0_ref = refs[1 + N_WEIGHTS:5 + N_WEIGHTS]
    h1_ref, u2p_ref, idx_ref, gate_ref = refs[5 + N_WEIGHTS:9 + N_WEIGHTS]
    ybuf, cbuf, zbuf, c_st, n_st, m_st = refs[9 + N_WEIGHTS:]
    t = SEQ_TILE

    @pl.when(pl.program_id(1) == 0)
    def _():
        ybuf[0:CONV_HIST, :] = y0_ref[...]
        c_st[...] = c0_ref[...]
        n_st[...] = n0_ref[...]
        m_st[...] = m0_ref[...]

    x = x_ref[0]
    u_bf = _rms_norm(x, w["g1"][...]).astype(BF16)

    ybuf[CONV_HIST:CONV_HIST + t, :] = _glu(u_bf, w, None)
    chunks = [(lo, min(lo + PROJ_CHUNK, end)) for start, end in ((C_Q, C_O), (C_O, N_PROJ))
              for lo in range(start, end, PROJ_CHUNK)]
    blocks = [(c, j) for c in range(t // CONV_ROWS) for j in range(D_MODEL // LANES)]
    every = len(blocks) // len(chunks)
    assert every >= 1
    for n, (c, j) in enumerate(blocks):
        cbuf[c * CONV_ROWS:(c + 1) * CONV_ROWS, j * LANES:(j + 1) * LANES] = _conv_block(
            ybuf, w["dww"], w["dwb"], c, j)
        if n % every == 0 and n // every < len(chunks):
            lo, hi = chunks[n // every]
            zbuf[:, lo - C_Q:hi - C_Q] = _proj_nobias(u_bf, w, lo, hi)
    ybuf[0:CONV_HIST, :] = ybuf[t:t + CONV_HIST, :]
    zcol = lambda lo, hi: zbuf[:, lo - C_Q:hi - C_Q] + w["bin"][:, lo:hi]

    conv = cbuf[...]
    mu = jnp.mean(conv, axis=-1, keepdims=True)
    cen = conv - mu
    var = jnp.mean(cen * cen, axis=-1, keepdims=True)
    ln = cen * lax.rsqrt(var + NORM_EPS) * w["lng"][...] + w["lnb"][...]
    conv_out = jnp.dot((ln * _sigmoid(ln)).astype(BF16), w["pww"][...],
                       preferred_element_type=F32) + w["pwb"][...]

    hcat = _mlstm(zcol, w, c_st, n_st, m_st, None, want_h=True)
    o_gate = _sigmoid(zcol(C_O, C_GC))
    mlstm_out = jnp.dot((o_gate * hcat).astype(BF16), w["mow"][...], preferred_element_type=F32)

    g_conv = _sigmoid(zcol(C_GC, C_GM))
    g_mlstm = _sigmoid(zcol(C_GM, N_PROJ))
    mix = (g_conv * conv_out + g_mlstm * mlstm_out).astype(BF16)
    h1 = x + jnp.dot(mix, w["wout"][...], preferred_element_type=F32)
    h1_ref[0] = h1

    u2 = _rms_norm(h1, w["g2"][...])
    _pack_rows(u2, u2p_ref)
    idx, gates = _router(u2, w["rw"], w["rb"])
    idx_ref[...] = idx
    gate_ref[0] = gates


def _const_spec(shape):
    nd = len(shape)
    return pl.BlockSpec(shape, lambda *_: (0,) * nd, pipeline_mode=pl.Buffered(1))


def _mixer_weights(norm_mix_g, w_in, b_in, conv_dw_w, conv_dw_b, conv_ln_g, conv_ln_b, conv_pw_w,
                   conv_pw_b, mlstm_norm_g, mlstm_out_w, w_out, norm_ffn_g, router_w, router_b):
    d = D_MODEL
    oo = 3 * d + 2 * M_HEADS * M_DK + 2 * M_HEADS
    gate_pad = LANES - 2 * M_HEADS
    w_in0 = w_in[0]
    win_a = jnp.pad(w_in0[:, :oo].astype(BF16), ((0, 0), (0, gate_pad)))
    win_b = w_in0[:, oo:].astype(BF16)

    def regroup(a):
        return jnp.concatenate([a[:oo], jnp.zeros((gate_pad,), a.dtype), a[oo:]])

    row = lambda a: a.reshape(1, -1).astype(F32)
    dww = jnp.pad(conv_dw_w[0].astype(F32), ((0, 32 - CONV_WIDTH), (0, 0)))
    rw = jnp.pad(router_w[0].astype(F32), ((0, 0), (0, LANES - N_EXPERTS)))
    rw_hi = rw.astype(BF16)
    rw_mid = (rw - rw_hi.astype(F32)).astype(BF16)
    rb = jnp.broadcast_to(router_b[0].astype(F32)[:, None], (N_EXPERTS, LANES))
    return [row(norm_mix_g[0]), win_a, win_b, row(regroup(b_in[0])),
            dww, row(conv_dw_b[0]), row(conv_ln_g[0]), row(conv_ln_b[0]),
            conv_pw_w[0].astype(BF16), row(conv_pw_b[0]), row(mlstm_norm_g[0]),
            mlstm_out_w[0].astype(BF16), w_out[0].astype(BF16), row(norm_ffn_g[0]),
            jnp.concatenate([rw_hi, rw_hi, rw_mid], axis=0).T, rb]


def _mixer(x, meta_tokens, weights):
    bsz, seq, d = x.shape
    t = SEQ_TILE
    n_seq = seq // t
    w_specs = [_const_spec(a.shape) for a in weights]
    state_shapes = [(CONV_HIST, d), (M_HEADS, M_DK, M_DV), (M_HEADS, SUBLANES, M_DK),
                    (M_HEADS, SUBLANES, LANES)]

    x_meta = jnp.concatenate([jnp.zeros((t - N_META, d), F32), meta_tokens.astype(F32)], axis=0)
    state = pl.pallas_call(
        _prefix_kernel,
        grid=(1,),
        in_specs=[pl.BlockSpec((t, d), lambda i: (0, 0))] + w_specs,
        out_specs=[pl.BlockSpec(s, lambda i, n=len(s): (0,) * n) for s in state_shapes],
        out_shape=[jax.ShapeDtypeStruct(s, F32) for s in state_shapes],
        compiler_params=pltpu.CompilerParams(vmem_limit_bytes=VMEM_LIMIT),
        name="prefix",
    )(x_meta, *weights)

    tile = lambda b, s: (b, s, 0)
    return pl.pallas_call(
        _mixer_kernel,
        grid=(bsz, n_seq),
        in_specs=([pl.BlockSpec((1, t, d), tile)] + w_specs
                  + [_const_spec(s) for s in state_shapes]),
        out_specs=[pl.BlockSpec((1, t, d), tile),
                   pl.BlockSpec((t * PACK_ROWS, LANES), lambda b, s: (b * n_seq + s, 0)),
                   pl.BlockSpec((SUBLANES, t), lambda b, s: (0, b * n_seq + s)),
                   pl.BlockSpec((1, t, LANES), tile)],
        out_shape=[jax.ShapeDtypeStruct((bsz, seq, d), F32),
                   jax.ShapeDtypeStruct((bsz * seq * PACK_ROWS, LANES), U32),
                   jax.ShapeDtypeStruct((SUBLANES, bsz * seq), jnp.int32),
                   jax.ShapeDtypeStruct((bsz, seq, LANES), F32)],
        scratch_shapes=[pltpu.VMEM((CONV_HIST + t, d), F32),
                        pltpu.VMEM((t, d), F32),
                        pltpu.VMEM((t, N_PROJ - C_Q), F32),
                        pltpu.VMEM((M_HEADS, M_DK, M_DV), F32),
                        pltpu.VMEM((M_HEADS, SUBLANES, M_DK), F32),
                        pltpu.VMEM((M_HEADS, SUBLANES, LANES), F32)],
        compiler_params=pltpu.CompilerParams(vmem_limit_bytes=VMEM_LIMIT,
                                             dimension_semantics=("arbitrary", "arbitrary")),
        name="mixer",
    )(x, *weights, *state)


def _chunk_rows(c):
    start = c * W_CHUNK
    return pl.ds(start if isinstance(c, int) else pl.multiple_of(start, W_CHUNK), W_CHUNK)


def _expert_kernel(cnt_ref, start_ref, total_ref, tok_ref, row_ref,
                   x_ref, w1_hbm, b1_ref, w2_hbm, b2_ref, ys_hbm,
                   tile, ybuf, wb1, wb2, st1, st2, osem, wsem1, wsem2):
    e = pl.program_id(0)
    n_exp = pl.num_programs(0)
    nb = cnt_ref[e]
    b0 = start_ref[e]
    total = total_ref[0]
    bm = MOE_BM
    n_planes_rows = ys_hbm.shape[0] - (N_EXPERTS + 1) * bm
    spare_block = n_planes_rows + N_EXPERTS * bm
    wcur = e % 2
    n_chunks = D_MODEL // W_CHUNK

    def chunk_copies(ex, c):
        rows = _chunk_rows(c)
        s = c % 2
        return (pltpu.make_async_copy(w1_hbm.at[ex, rows, :], st1.at[s], wsem1.at[s]),
                pltpu.make_async_copy(w2_hbm.at[ex, rows, :], st2.at[s], wsem2.at[s]))

    def chunk_start(ex, c):
        for cp in chunk_copies(ex, c):
            cp.start()

    def chunk_finish(ex, c, wslot):
        for cp in chunk_copies(ex, c):
            cp.wait()
        rows = _chunk_rows(c)
        wb1[wslot, rows, :] = st1[c % 2].astype(BF16)
        wb2[wslot, rows, :] = st2[c % 2].astype(BF16)

    def stream_next(c):
        c = jnp.asarray(c, jnp.int32)

        @pl.when(jnp.logical_and(c < n_chunks, e + 1 < n_exp))
        def _():
            chunk_finish(e + 1, c, 1 - wcur)

            @pl.when(c + 2 < n_chunks)
            def _():
                chunk_start(e + 1, c + 2)

    def gather(blk, slot):
        for r in range(bm):
            src = pl.multiple_of(tok_ref[blk * bm + r], PACK_ROWS)
            tile[slot, pl.ds(PACK_ROWS * r, PACK_ROWS), :] = x_ref[pl.ds(src, PACK_ROWS), :]

    def scatter_start(blk, slot):
        for r in range(bm):
            dst = row_ref[blk * bm + r]
            pltpu.make_async_copy(ybuf.at[slot, pl.ds(r, 1), :], ys_hbm.at[pl.ds(dst, 1), :],
                                  osem.at[slot]).start(priority=r % 2)

    def block_copy(slot, row0):
        return pltpu.make_async_copy(ybuf.at[slot], ys_hbm.at[pl.ds(row0, bm), :], osem.at[slot])

    def scatter_wait(slot):
        block_copy(slot, 0).wait()

    def mlp(slot):
        xb = _unpack_rows(tile, slot, bm)
        hcat = jnp.dot(xb, wb1[wcur], preferred_element_type=F32) + b1_ref[0]
        h_glu = jnp.minimum(hcat[:, :D_FF], SWIGLU_LIMIT)
        h_lin = jnp.clip(hcat[:, D_FF:], -SWIGLU_LIMIT, SWIGLU_LIMIT)
        act = h_glu * _sigmoid(SWIGLU_ALPHA * h_glu) * (h_lin + 1.0)
        y = jnp.dot(act.astype(BF16), wb2[wcur], preferred_element_type=F32) + b2_ref[0]
        bits = lax.bitcast_convert_type(y.astype(BF16).astype(F32), U32)
        return bits[:, D_MODEL // PACK:] | (bits[:, :D_MODEL // PACK] >> 16)

    @pl.when(e == 0)
    def _():
        chunk_start(0, 0)
        chunk_start(0, 1)
        for c in range(n_chunks):
            chunk_finish(0, c, 0)
            if c + 2 < n_chunks:
                chunk_start(0, c + 2)
        ybuf[...] = jnp.zeros(ybuf.shape, U32)
        for i in range(N_EXPERTS):
            block_copy(0, n_planes_rows + i * bm).start()
        for i in range(N_EXPERTS):
            block_copy(0, n_planes_rows + i * bm).wait()

    @pl.when(e + 1 < n_exp)
    def _():
        chunk_start(e + 1, 0)
        chunk_start(e + 1, 1)

    @pl.when(nb > 0)
    def _():
        @pl.when(b0 == 0)
        def _():
            gather(0, 0)
            y = mlp(0)
            gather(1, 1)
            ybuf[0] = y
            block_copy(1, spare_block).start()
            stream_next(0)

        def block_body(blk, slot):
            y = mlp(slot)
            gather(blk + 1, 1 - slot)
            scatter_start(blk - 1, 1 - slot)
            scatter_wait(slot)
            ybuf[slot] = y

        def block(i, carry):
            blk = b0 + i
            for slot in range(2):
                @pl.when(blk % 2 == slot)
                def _():
                    block_body(blk, slot)
            stream_next(i)
            return carry

        lax.fori_loop(jnp.where(b0 == 0, 1, 0), nb, block, 0)

    lax.fori_loop(jnp.minimum(nb, n_chunks), n_chunks, lambda c, carry: (stream_next(c), carry)[1], 0)

    @pl.when(e == pl.num_programs(0) - 1)
    def _():
        for last in range(2):
            @pl.when((total - 1) % 2 == last)
            def _():
                scatter_start(total - 1, last)
        scatter_wait(0)
        scatter_wait(1)


def _experts(u2p, blk_cnt, blk_start, blk_total, slot_tok, slot_row, n_tok, w1, b1, w2, b2):
    d = D_MODEL
    emap = lambda e, *_: (e, 0, 0)
    n_rows = TOP_K * n_tok + (N_EXPERTS + 1) * MOE_BM
    grid_spec = pltpu.PrefetchScalarGridSpec(
        num_scalar_prefetch=5,
        grid=(N_EXPERTS,),
        in_specs=[pl.BlockSpec(u2p.shape, lambda e, *_: (0, 0), pipeline_mode=pl.Buffered(1)),
                  pl.BlockSpec(memory_space=pl.ANY),
                  pl.BlockSpec((1, 1, 2 * D_FF), emap),
                  pl.BlockSpec(memory_space=pl.ANY),
                  pl.BlockSpec((1, 1, d), emap)],
        out_specs=pl.BlockSpec(memory_space=pl.ANY),
        scratch_shapes=[pltpu.VMEM((2, MOE_BM * PACK_ROWS, LANES), U32),
                        pltpu.VMEM((2, MOE_BM, d // PACK), U32),
                        pltpu.VMEM((2, d, 2 * D_FF), BF16),
                        pltpu.VMEM((2, D_FF, d), BF16),
                        pltpu.VMEM((2, W_CHUNK, 2 * D_FF), F32),
                        pltpu.VMEM((2, W_CHUNK, d), F32),
                        pltpu.SemaphoreType.DMA((2,)),
                        pltpu.SemaphoreType.DMA((2,)),
                        pltpu.SemaphoreType.DMA((2,))],
    )
    return pl.pallas_call(
        _expert_kernel,
        grid_spec=grid_spec,
        out_shape=jax.ShapeDtypeStruct((n_rows, d // PACK), U32),
        compiler_params=pltpu.CompilerParams(vmem_limit_bytes=VMEM_LIMIT,
                                             dimension_semantics=("arbitrary",)),
        name="experts",
    )(blk_cnt, blk_start, blk_total, slot_tok, slot_row, u2p,
      w1.astype(F32), b1.reshape(N_EXPERTS, 1, 2 * D_FF), w2.astype(F32),
      b2.reshape(N_EXPERTS, 1, d))


def _combine_kernel(h1_ref, gate_ref, g_ref, y0_ref, y1_ref, y2_ref, y3_ref, out_ref):
    acc = h1_ref[...]
    gates = gate_ref[...]
    for kk, y_ref in enumerate((y0_ref, y1_ref, y2_ref, y3_ref)):
        word = y_ref[...]
        lo = lax.bitcast_convert_type(word << 16, F32)
        hi = lax.bitcast_convert_type(word & jnp.uint32(0xFFFF0000), F32)
        acc = acc + gates[:, kk:kk + 1] * jnp.concatenate([lo, hi], axis=1)
    out_ref[...] = _rms_norm(acc, g_ref[...])


def _combine(h1, gates, final_g, ys):
    n, d = h1.shape
    tt = COMB_TILE
    nt = n // tt
    plane = lambda kk: pl.BlockSpec((tt, d // PACK), lambda i, kk=kk: (kk * nt + i, 0))
    return pl.pallas_call(
        _combine_kernel,
        grid=(nt,),
        in_specs=[pl.BlockSpec((tt, d), lambda i: (i, 0)),
                  pl.BlockSpec((tt, LANES), lambda i: (i, 0)),
                  pl.BlockSpec((1, d), lambda i: (0, 0))] + [plane(kk) for kk in range(TOP_K)],
        out_specs=pl.BlockSpec((tt, d), lambda i: (i, 0)),
        out_shape=jax.ShapeDtypeStruct((n, d), F32),
        compiler_params=pltpu.CompilerParams(vmem_limit_bytes=VMEM_LIMIT,
                                             dimension_semantics=("arbitrary",)),
        name="combine",
    )(h1, gates, final_g.reshape(1, d).astype(F32), ys, ys, ys, ys)


def _routing_tables(top_idx):
    n_tok = top_idx.shape[1]
    n_asg = n_tok * TOP_K
    bm = MOE_BM
    flat_e = top_idx.reshape(-1).astype(jnp.int32)
    sorted_e, order = lax.sort((flat_e, jnp.arange(n_asg, dtype=jnp.int32)), num_keys=1)
    experts = jnp.arange(N_EXPERTS, dtype=jnp.int32)
    counts = jnp.sum((flat_e[:, None] == experts[None, :]).astype(jnp.int32), axis=0)
    grp_start = jnp.cumsum(counts).astype(jnp.int32) - counts
    blk_cnt = (counts + bm - 1) // bm
    blk_end = jnp.cumsum(blk_cnt).astype(jnp.int32)
    blk_start = blk_end - blk_cnt
    n_blocks = n_asg // bm + N_EXPERTS
    blk = jnp.arange(n_blocks, dtype=jnp.int32)
    blk_e = jnp.minimum(jnp.sum((blk[:, None] >= blk_end[None, :]).astype(jnp.int32), axis=1),
                        N_EXPERTS - 1)
    e_count = counts[blk_e][:, None]
    e_first = grp_start[blk_e][:, None]
    within = ((blk - blk_start[blk_e]) * bm)[:, None] + jnp.arange(bm, dtype=jnp.int32)[None, :]
    real = within < e_count
    asg = order[jnp.where(real, e_first + within, 0)]
    tok = asg % n_tok
    pad_ord = blk[:, None] * bm + jnp.arange(bm, dtype=jnp.int32)[None, :] - (
        e_first + jnp.minimum(within, e_count))
    slot_tok = jnp.where(real, tok * PACK_ROWS, 0).astype(jnp.int32).reshape(-1)
    slot_row = jnp.where(real, asg,
                         n_asg + pad_ord).astype(jnp.int32).reshape(-1)
    return blk_cnt, blk_start, blk_end[-1:], slot_tok, slot_row


def kernel(x, meta_tokens, norm_mix_g, w_in, b_in, conv_dw_w, conv_dw_b, conv_ln_g, conv_ln_b, conv_pw_w, conv_pw_b, mlstm_norm_g, mlstm_out_w, w_out, norm_ffn_g, router_w, router_b, expert_w1, expert_b1, expert_w2, expert_b2, final_norm_g):
    bsz, seq, d = x.shape
    n_tok = bsz * seq
    assert d == D_MODEL and seq % SEQ_TILE == 0 and n_tok % COMB_TILE == 0 and w_in.shape[0] == 1
    weights = _mixer_weights(norm_mix_g, w_in, b_in, conv_dw_w, conv_dw_b, conv_ln_g, conv_ln_b,
                             conv_pw_w, conv_pw_b, mlstm_norm_g, mlstm_out_w, w_out, norm_ffn_g,
                             router_w, router_b)
    h1, u2p, idx, gates = _mixer(x.astype(F32), meta_tokens, weights)
    h1 = h1.reshape(n_tok, d)
    gates = gates.reshape(n_tok, LANES)
    top_idx = idx[:TOP_K]
    blk_cnt, blk_start, blk_total, slot_tok, slot_row = _routing_tables(top_idx)
    ys = _experts(u2p, blk_cnt, blk_start, blk_total, slot_tok, slot_row, n_tok,
                  expert_w1[0], expert_b1[0], expert_w2[0], expert_b2[0])
    out = _combine(h1, gates, final_norm_g, ys)
    return out.reshape(bsz, seq, d)
```
